```python
import math
import jax, jax.numpy as jnp
from jax import lax
import numpy as np

D_MODEL = 1024
BATCH = 8
SEQ = 8192
DEPTH = 1

MEM_LEN = 256
HEAD_DIM = 64
N_FOX_HEADS = (D_MODEL // 2) // HEAD_DIM
N_MOBA_HEADS = (D_MODEL // 2) // HEAD_DIM
FOX_WIDTH = N_FOX_HEADS * HEAD_DIM
MOBA_WIDTH = N_MOBA_HEADS * HEAD_DIM
MIX_WIDTH = FOX_WIDTH + MOBA_WIDTH
IN_COLS = 3 * FOX_WIDTH + N_FOX_HEADS + 3 * MOBA_WIDTH
FOX_Q_BLOCK = 128
MOBA_BLOCK = 256
MOBA_TOP_K = 3
MOBA_Q_BLOCK = 32
N_CROSS_HEADS = 4
CROSS_HEAD_DIM = D_MODEL // N_CROSS_HEADS
D_FF = 128 * (-(-(8 * D_MODEL // 3) // 128))
CONV_WIDTH = 3
EPS = 1e-6

kernel_name = "hymba_fox_moba_convffn_layer"


def rmsnorm(x, g):
    xf = x.astype(jnp.float32)
    y = xf * lax.rsqrt(jnp.mean(xf * xf, axis=-1, keepdims=True) + EPS)
    return (y * g.astype(jnp.float32)).astype(x.dtype)


def to_heads(t, n_heads):
    b, s, _ = t.shape
    return t.reshape(b, s, n_heads, HEAD_DIM).transpose(0, 2, 1, 3)


def from_heads(t):
    b, h, s, d = t.shape
    return t.transpose(0, 2, 1, 3).reshape(b, s, h * d)


def fox_attention(q, k, v, log_f):
    B, H, S, dh = q.shape
    c = jnp.cumsum(log_f, axis=-1)
    scale = dh ** -0.5
    kpos = jnp.arange(S)

    def block(i):
        start = i * FOX_Q_BLOCK
        qb = lax.dynamic_slice_in_dim(q, start, FOX_Q_BLOCK, axis=2)
        cb = lax.dynamic_slice_in_dim(c, start, FOX_Q_BLOCK, axis=2)
        qpos = start + jnp.arange(FOX_Q_BLOCK)
        s = jnp.einsum('bhqd,bhkd->bhqk', qb, k, preferred_element_type=jnp.float32) * scale
        s = s + cb[..., :, None] - c[..., None, :]
        s = jnp.where(kpos[None, :] <= qpos[:, None], s, -jnp.inf)
        p = jax.nn.softmax(s, axis=-1)
        return jnp.einsum('bhqk,bhkd->bhqd', p.astype(v.dtype), v)

    out = lax.map(block, jnp.arange(S // FOX_Q_BLOCK))
    return jnp.moveaxis(out, 0, 2).reshape(B, H, S, dh)


def moba_attention(q, k, v, slopes):
    B, H, S, dh = q.shape
    n_blk = -(-S // MOBA_BLOCK)
    s_pad = n_blk * MOBA_BLOCK
    pad = ((0, 0), (0, 0), (0, s_pad - S), (0, 0))
    kb = jnp.pad(k, pad).reshape(B, H, n_blk, MOBA_BLOCK, dh)
    vb = jnp.pad(v, pad).reshape(B, H, n_blk, MOBA_BLOCK, dh)
    k_mean = jnp.mean(kb.astype(jnp.float32), axis=3)
    top_k = min(MOBA_TOP_K, n_blk)
    scale = dh ** -0.5
    offs = jnp.arange(MOBA_BLOCK)
    blk_ids = jnp.arange(n_blk)
    bidx = jnp.arange(B)[:, None, None, None]
    hidx = jnp.arange(H)[None, :, None, None]
    sl5 = slopes[None, :, None, None, None]
    sl4 = slopes[None, :, None, None]

    def chunk(i):
        start = i * MOBA_Q_BLOCK
        qb = lax.dynamic_slice_in_dim(q, start, MOBA_Q_BLOCK, axis=2)
        qpos = start + jnp.arange(MOBA_Q_BLOCK)
        own = start // MOBA_BLOCK
        gate = jnp.einsum('bhqd,bhnd->bhqn', qb.astype(jnp.float32), k_mean)
        gate = jnp.where(blk_ids < own, gate, -jnp.inf)
        _, sel = lax.top_k(gate, top_k)
        valid = sel < own
        k_sel = kb[bidx, hidx, sel]
        v_sel = vb[bidx, hidx, sel]
        s_sel = jnp.einsum('bhqd,bhqnkd->bhqnk', qb, k_sel,
                           preferred_element_type=jnp.float32) * scale
        dist_sel = (qpos[None, None, :, None, None]
                    - (sel[..., None] * MOBA_BLOCK + offs)).astype(jnp.float32)
        s_sel = jnp.where(valid[..., None], s_sel - sl5 * dist_sel, -jnp.inf)
        s_sel = s_sel.reshape(B, H, MOBA_Q_BLOCK, top_k * MOBA_BLOCK)
        k_own = kb[:, :, own]
        v_own = vb[:, :, own]
        s_own = jnp.einsum('bhqd,bhkd->bhqk', qb, k_own,
                           preferred_element_type=jnp.float32) * scale
        dist_own = (qpos[:, None] - (own * MOBA_BLOCK + offs)[None, :]).astype(jnp.float32)
        s_own = jnp.where(dist_own >= 0, s_own - sl4 * dist_own, -jnp.inf)
        p = jax.nn.softmax(jnp.concatenate([s_sel, s_own], axis=-1), axis=-1).astype(v.dtype)
        p_sel = p[..., :top_k * MOBA_BLOCK].reshape(B, H, MOBA_Q_BLOCK, top_k, MOBA_BLOCK)
        p_own = p[..., top_k * MOBA_BLOCK:]
        return (jnp.einsum('bhqnk,bhqnkd->bhqd', p_sel, v_sel)
                + jnp.einsum('bhqk,bhkd->bhqd', p_own, v_own))

    out = lax.map(chunk, jnp.arange(S // MOBA_Q_BLOCK))
    return jnp.moveaxis(out, 0, 2).reshape(B, H, S, dh)


def alibi_slopes(n_heads):
    return 2.0 ** (-8.0 * jnp.arange(1, n_heads + 1, dtype=jnp.float32) / n_heads)


def hybrid_mixer(a, w_in, b_forget, g_fox, g_moba, w_out):
    proj = a @ w_in
    cuts = [int(c) for c in np.cumsum([FOX_WIDTH, FOX_WIDTH, FOX_WIDTH, N_FOX_HEADS,
                                       MOBA_WIDTH, MOBA_WIDTH])]
    q_f, k_f, v_f, f_logit, q_m, k_m, v_m = jnp.split(proj, cuts, axis=-1)
    log_f = jax.nn.log_sigmoid(f_logit.astype(jnp.float32)
                               + b_forget.astype(jnp.float32)).transpose(0, 2, 1)
    o_f = fox_attention(to_heads(q_f, N_FOX_HEADS), to_heads(k_f, N_FOX_HEADS),
                        to_heads(v_f, N_FOX_HEADS), log_f)
    o_m = moba_attention(to_heads(q_m, N_MOBA_HEADS), to_heads(k_m, N_MOBA_HEADS),
                         to_heads(v_m, N_MOBA_HEADS), alibi_slopes(N_MOBA_HEADS))
    y = jnp.concatenate([rmsnorm(from_heads(o_f), g_fox),
                         rmsnorm(from_heads(o_m), g_moba)], axis=-1)
    return y @ w_out


def memory_cross_attention(a, mem_n, w_q, w_kv, w_o):
    B, S, _ = a.shape
    M = mem_n.shape[1]
    q = (a @ w_q).reshape(B, S, N_CROSS_HEADS, CROSS_HEAD_DIM)
    k, v = jnp.split(mem_n @ w_kv, 2, axis=-1)
    k = k.reshape(B, M, N_CROSS_HEADS, CROSS_HEAD_DIM)
    v = v.reshape(B, M, N_CROSS_HEADS, CROSS_HEAD_DIM)
    s = jnp.einsum('bshd,bmhd->bhsm', q, k,
                   preferred_element_type=jnp.float32) * CROSS_HEAD_DIM ** -0.5
    p = jax.nn.softmax(s, axis=-1).astype(v.dtype)
    o = jnp.einsum('bhsm,bmhd->bshd', p, v).reshape(B, S, D_MODEL)
    return o @ w_o


def conv_glu_ffn(a, w_up, conv_w, conv_b, w_down):
    S = a.shape[1]
    g, u = jnp.split(a @ w_up, 2, axis=-1)
    gp = jnp.pad(g, ((0, 0), (CONV_WIDTH - 1, 0), (0, 0)))
    gc = conv_b + sum(conv_w[j] * gp[:, j:j + S] for j in range(CONV_WIDTH))
    return (jax.nn.silu(gc) * u) @ w_down


def setup_inputs(seed: int = 0) -> dict:
    key = jax.random.key(seed)
    ks = jax.random.split(key, 24)
    f32 = jnp.float32
    nrm = lambda k, shape, s: jax.random.normal(k, shape, f32) * s
    gain = lambda k, shape: 1.0 + 0.02 * jax.random.normal(k, shape, f32)
    L = DEPTH
    return {
        "x": jax.random.normal(ks[0], (BATCH, SEQ, D_MODEL), f32),
        "mem": jax.random.normal(ks[1], (BATCH, MEM_LEN, D_MODEL), f32),
        "g_mix": gain(ks[2], (L, D_MODEL)),
        "w_in": nrm(ks[3], (L, D_MODEL, IN_COLS), D_MODEL ** -0.5),
        "b_forget": 3.0 + 0.5 * jax.random.normal(ks[4], (L, N_FOX_HEADS), f32),
        "g_fox": gain(ks[5], (L, FOX_WIDTH)),
        "g_moba": gain(ks[6], (L, MOBA_WIDTH)),
        "w_out": nrm(ks[7], (L, MIX_WIDTH, D_MODEL), MIX_WIDTH ** -0.5),
        "g_cross": gain(ks[8], (L, D_MODEL)),
        "g_mem": gain(ks[9], (L, D_MODEL)),
        "w_q_mem": nrm(ks[10], (L, D_MODEL, D_MODEL), D_MODEL ** -0.5),
        "w_kv_mem": nrm(ks[11], (L, D_MODEL, 2 * D_MODEL), D_MODEL ** -0.5),
        "w_o_mem": nrm(ks[12], (L, D_MODEL, D_MODEL), D_MODEL ** -0.5),
        "g_ffn": gain(ks[13], (L, D_MODEL)),
        "w_up": nrm(ks[14], (L, D_MODEL, 2 * D_FF), D_MODEL ** -0.5),
        "conv_w": nrm(ks[15], (L, CONV_WIDTH, D_FF), CONV_WIDTH ** -0.5),
        "conv_b": nrm(ks[16], (L, D_FF), 0.02),
        "w_down": nrm(ks[17], (L, D_FF, D_MODEL), D_FF ** -0.5),
        "g_final": gain(ks[18], (D_MODEL,)),
    }


def reference(x, mem, g_mix, w_in, b_forget, g_fox, g_moba, w_out, g_cross, g_mem,
              w_q_mem, w_kv_mem, w_o_mem, g_ffn, w_up, conv_w, conv_b, w_down, g_final):
    h = x
    for l in range(DEPTH):
        h = h + hybrid_mixer(rmsnorm(h, g_mix[l]), w_in[l], b_forget[l],
                             g_fox[l], g_moba[l], w_out[l])
        h = h + memory_cross_attention(rmsnorm(h, g_cross[l]), rmsnorm(mem, g_mem[l]),
                                       w_q_mem[l], w_kv_mem[l], w_o_mem[l])
        h = h + conv_glu_ffn(rmsnorm(h, g_ffn[l]), w_up[l], conv_w[l], conv_b[l], w_down[l])
    return rmsnorm(h, g_final)
```

```python
import functools

import numpy as np
import jax
import jax.numpy as jnp
from jax import lax
from jax.experimental import pallas as pl
from jax.experimental.pallas import tpu as pltpu

F32 = jnp.float32
BF16 = jnp.bfloat16

D_MODEL = 1024
HEAD_DIM = 64
N_HEADS = 8
GROUP_WIDTH = N_HEADS * HEAD_DIM
MOBA_BLOCK = 256
MOBA_TOP_K = 3
MAX_MOBA_BLOCKS = 32
N_CROSS_HEADS = 4
CROSS_HEAD_DIM = D_MODEL // N_CROSS_HEADS
CONV_WIDTH = 3
EPS = 1e-6

LANES = 128
SUBLANES = 8
AUG = N_HEADS * LANES
FLOGIT_PAD = LANES
MASK_NEG = -2.0 ** 100
BIAS_LANE = 0
MASK_LANE = 8
FFN_CHUNK = 256
VMEM_LIMIT = 56 * 1024 * 1024


def _extras_base(h):
    return h * LANES + (HEAD_DIM if h % 2 == 0 else 0)


def _lane_tables(slopes):
    own = np.zeros((AUG,), np.float32)
    sign_q = np.zeros((AUG,), np.float32)
    sign_k = np.zeros((AUG,), np.float32)
    head_of = np.zeros((AUG,), np.int32)
    piece = np.zeros((AUG,), np.float32)
    ones_q = np.zeros((AUG,), np.float32)
    ones_k = np.zeros((AUG,), np.float32)
    blk = np.full((AUG,), -1.0, np.float32)
    for h in range(N_HEADS):
        head_of[h * LANES:(h + 1) * LANES] = h
        q0 = h * LANES + (0 if h % 2 == 0 else HEAD_DIM)
        own[q0:q0 + HEAD_DIM] = 1.0
        b = _extras_base(h) + BIAS_LANE
        for k in range(3):
            sign_q[b + k] = -1.0
            ones_k[b + k] = 1.0
            ones_q[b + 3 + k] = 1.0
            sign_k[b + 3 + k] = 1.0
            piece[b + k] = k
            piece[b + 3 + k] = k
        m = _extras_base(h) + MASK_LANE
        blk[m:m + MAX_MOBA_BLOCKS] = np.arange(MAX_MOBA_BLOCKS)
    slope_lane = slopes.astype(F32)[head_of]
    rows = [jnp.asarray(own), jnp.asarray(sign_q) * slope_lane, jnp.asarray(sign_k) * slope_lane,
            jnp.asarray(piece), jnp.asarray(ones_q), jnp.asarray(ones_k), jnp.asarray(blk),
            jnp.zeros((AUG,), F32)]
    return jnp.stack(rows, axis=0)


def _placement_matrices():
    p_fox = np.zeros((3 * LANES, 2 * AUG), np.float32)
    p_mask = np.zeros((N_HEADS * MAX_MOBA_BLOCKS, AUG), np.float32)
    for h in range(N_HEADS):
        b = _extras_base(h) + BIAS_LANE
        for k in range(3):
            p_fox[k * LANES + h, b + k] = 1.0
            p_fox[k * LANES + h, AUG + b + 3 + k] = -1.0
        m = _extras_base(h) + MASK_LANE
        for j in range(MAX_MOBA_BLOCKS):
            p_mask[h * MAX_MOBA_BLOCKS + j, m + j] = 1.0
    return jnp.asarray(p_fox, BF16), jnp.asarray(p_mask, BF16)


def _split3(v):
    hi = v.astype(BF16)
    r = v - hi.astype(F32)
    mid = r.astype(BF16)
    lo = (r - mid.astype(F32)).astype(BF16)
    return hi, mid, lo


def _split2(v):
    hi = v.astype(BF16)
    return hi, (v - hi.astype(F32)).astype(BF16)


def _rms(x, g):
    return x * lax.rsqrt(jnp.mean(x * x, axis=-1, keepdims=True) + EPS) * g


def _dot(a, b):
    return jnp.dot(a, b, preferred_element_type=F32)


def _dot_nt(a, b):
    return lax.dot_general(a, b, (((1,), (1,)), ((), ())), preferred_element_type=F32)


def _dup_pairs(t):
    cols = []
    for p in range(GROUP_WIDTH // LANES):
        c = t[:, p * LANES:(p + 1) * LANES]
        cols += [c, c]
    return jnp.concatenate(cols, axis=1)


def _const_spec(shape):
    return pl.BlockSpec(shape, lambda *_: (0,) * len(shape), pipeline_mode=pl.Buffered(1))


def _inproj_kernel(x_ref, g_ref, w_ref, bf_ref, tri_ref, pfox_ref, pmask_ref, meta_ref,
                   qf_ref, kf_ref, vf_ref, qm_ref, km_ref, vm_ref, carry_ref, kbdt_ref):
    tm = MOBA_BLOCK
    i = pl.program_id(1)

    @pl.when(i == 0)
    def _():
        carry_ref[...] = jnp.zeros_like(carry_ref)
        kbdt_ref[...] = jnp.zeros_like(kbdt_ref)

    a = _rms(x_ref[0], g_ref[...]).astype(BF16)

    def proj(c0, n):
        return _dot(a, w_ref[:, c0:c0 + n])

    meta = meta_ref[...]
    own = meta[0:1] > 0.5
    piece_idx = meta[3:4]
    ones_q = meta[4:5]
    ones_k = meta[5:6]

    z = proj(6 * GROUP_WIDTH, FLOGIT_PAD) + bf_ref[...]
    log_f = jnp.minimum(z, 0.0) - jnp.log1p(jnp.exp(-jnp.abs(z)))
    tri = tri_ref[...]
    f_hi, f_mid, f_lo = _split3(log_f)
    c = (_dot(tri, f_hi) + _dot(tri, f_mid) + _dot(tri, f_lo)) + carry_ref[...]
    carry_ref[...] = c[tm - 1:tm, :]
    ext = _dot(jnp.concatenate(_split3(c), axis=1), pfox_ref[...])
    qf = proj(0, GROUP_WIDTH) * (HEAD_DIM ** -0.5)
    kf = proj(GROUP_WIDTH, GROUP_WIDTH)
    qf_ref[0] = jnp.where(own, _dup_pairs(qf), ext[:, :AUG] + ones_q).astype(BF16)
    kf_ref[0] = jnp.where(own, _dup_pairs(kf), ext[:, AUG:] + ones_k).astype(BF16)
    vf_ref[0] = proj(2 * GROUP_WIDTH, GROUP_WIDTH).astype(BF16)

    qm = proj(3 * GROUP_WIDTH, GROUP_WIDTH)
    km = proj(4 * GROUP_WIDTH, GROUP_WIDTH)
    vm_ref[0] = proj(5 * GROUP_WIDTH, GROUP_WIDTH).astype(BF16)

    kb_hi, kb_lo = _split2(kbdt_ref[...])
    q_hi, q_lo = _split2(qm)
    gate_t = _dot_nt(kb_hi, q_hi) + _dot_nt(kb_hi, q_lo) + _dot_nt(kb_lo, q_hi)

    jrow = lax.broadcasted_iota(jnp.int32, (MAX_MOBA_BLOCKS, tm), 0)
    past = jrow < i
    masks = []
    for h in range(N_HEADS):
        g = jnp.where(past, gate_t[h * MAX_MOBA_BLOCKS:(h + 1) * MAX_MOBA_BLOCKS], -jnp.inf)
        sel = jrow == i
        for _ in range(MOBA_TOP_K):
            top = jnp.max(g, axis=0, keepdims=True)
            cand = past & (g == top)
            first = jnp.min(jnp.where(cand, jrow, MAX_MOBA_BLOCKS), axis=0, keepdims=True)
            pick = jrow == first
            sel = sel | pick
            g = jnp.where(pick, -jnp.inf, g)
        masks.append(jnp.where(sel, 0.0, MASK_NEG))
    mask = jnp.concatenate(masks, axis=0).T.astype(BF16)
    ext_mask = _dot(mask, pmask_ref[...])

    pos = (i * tm + lax.broadcasted_iota(jnp.int32, (tm, 1), 0)).astype(F32)

    def bias_pieces(full):
        hi = full.astype(BF16).astype(F32)
        r = full - hi
        mid = r.astype(BF16).astype(F32)
        return jnp.where(piece_idx == 0.0, hi, jnp.where(piece_idx == 1.0, mid, r - mid))

    ext_q = ext_mask + bias_pieces(pos * meta[1:2]) + ones_q
    onehot = jnp.where(meta[6:7] == i.astype(F32), 1.0, 0.0)
    ext_k = bias_pieces(pos * meta[2:3]) + ones_k + onehot
    qm_ref[0] = jnp.where(own, _dup_pairs(qm * (HEAD_DIM ** -0.5)), ext_q).astype(BF16)
    km_ref[0] = jnp.where(own, _dup_pairs(km), ext_k).astype(BF16)

    kmean = jnp.mean(km, axis=0, keepdims=True)
    head_of_lane = lax.broadcasted_iota(jnp.int32, (1, GROUP_WIDTH), 1) // HEAD_DIM
    for h in range(N_HEADS):
        kbdt_ref[pl.ds(h * MAX_MOBA_BLOCKS + i, 1), :] = jnp.where(head_of_lane == h, kmean, 0.0)


def _in_projection(x, g_mix, w_in, b_forget, slopes):
    B, S, D = x.shape
    tm = MOBA_BLOCK
    n_blk = S // tm
    assert S % tm == 0 and MOBA_TOP_K <= n_blk <= MAX_MOBA_BLOCKS and D == D_MODEL
    c = np.cumsum([GROUP_WIDTH, GROUP_WIDTH, GROUP_WIDTH, N_HEADS, GROUP_WIDTH, GROUP_WIDTH])
    w_f = jnp.pad(w_in[:, c[2]:c[3]], ((0, 0), (0, FLOGIT_PAD - N_HEADS)))
    w_r = jnp.concatenate([w_in[:, :c[2]], w_in[:, c[3]:], w_f], axis=1).astype(BF16)
    n_cols = w_r.shape[1]
    b_f = jnp.pad(b_forget.astype(F32), (0, FLOGIT_PAD - N_HEADS)).reshape(1, FLOGIT_PAD)
    tri = jnp.asarray(np.tril(np.ones((tm, tm), np.float32)), BF16)
    p_fox, p_mask = _placement_matrices()
    meta = _lane_tables(slopes)

    aug = jax.ShapeDtypeStruct((B, S, AUG), BF16)
    val = jax.ShapeDtypeStruct((B, S, GROUP_WIDTH), BF16)
    tile = lambda w: pl.BlockSpec((1, tm, w), lambda b, i: (b, i, 0))
    return pl.pallas_call(
        _inproj_kernel,
        grid=(B, n_blk),
        in_specs=[tile(D), _const_spec((1, D)), _const_spec((D, n_cols)), _const_spec((1, FLOGIT_PAD)),
                  _const_spec((tm, tm)), _const_spec(p_fox.shape), _const_spec(p_mask.shape),
                  _const_spec(meta.shape)],
        out_specs=[tile(AUG), tile(AUG), tile(GROUP_WIDTH), tile(AUG), tile(AUG), tile(GROUP_WIDTH)],
        out_shape=[aug, aug, val, aug, aug, val],
        scratch_shapes=[pltpu.VMEM((1, FLOGIT_PAD), F32),
                        pltpu.VMEM((N_HEADS * MAX_MOBA_BLOCKS, GROUP_WIDTH), F32)],
        compiler_params=pltpu.CompilerParams(dimension_semantics=("arbitrary", "arbitrary"),
                                             vmem_limit_bytes=VMEM_LIMIT),
        name="in_projection",
    )(x, g_mix.reshape(1, D), w_r, b_f, tri, p_fox, p_mask, meta)


def _attention_kernel(q_ref, k_ref, v_ref, o_ref, va_ref, vb_ref, m_ref, acc_ref, *, seq, t):
    low = lax.broadcasted_iota(jnp.int32, (1, LANES), 1) < HEAD_DIM
    vp = v_ref[0]
    one = jnp.ones((), BF16)
    va_ref[...] = jnp.where(low, vp, one)
    vb_ref[...] = jnp.where(low, one, vp)
    v_refs = (va_ref, vb_ref)
    causal = (lax.broadcasted_iota(jnp.int32, (t, t), 1) <= lax.broadcasted_iota(jnp.int32, (t, t), 0))
    reps = t // LANES

    def q_tile(qi, carry):
        q0 = pl.multiple_of(qi * t, t)
        qs = [q_ref[0, pl.ds(q0, t), hh * LANES:(hh + 1) * LANES] for hh in range(2)]
        for hh in range(2):
            s = _dot_nt(qs[hh], k_ref[0, pl.ds(q0, t), hh * LANES:(hh + 1) * LANES])
            s = jnp.where(causal, s, -jnp.inf)
            m = jnp.max(s, axis=1, keepdims=True)
            p = jnp.exp(s - m)
            acc_ref[hh] = _dot(p.astype(BF16), v_refs[hh][pl.ds(q0, t), :])
            m_ref[hh] = jnp.broadcast_to(m, (t, LANES))

        def kv_tile(j, c):
            k0 = pl.multiple_of(j * t, t)
            for hh in range(2):
                s = _dot_nt(qs[hh], k_ref[0, pl.ds(k0, t), hh * LANES:(hh + 1) * LANES])
                m_prev = m_ref[hh]
                m_next = jnp.maximum(m_prev, jnp.max(s, axis=1, keepdims=True))
                p = jnp.exp(s - jnp.concatenate([m_next] * reps, axis=1))
                alpha = jnp.exp(m_prev - m_next)
                acc_ref[hh] = alpha * acc_ref[hh] + _dot(p.astype(BF16), v_refs[hh][pl.ds(k0, t), :])
                m_ref[hh] = m_next
            return c

        lax.fori_loop(0, qi, kv_tile, 0)
        outs = []
        for hh in range(2):
            acc = acc_ref[hh]
            outs.append(acc / pltpu.roll(acc, HEAD_DIM, 1))
        o_ref[0, pl.ds(q0, t), :] = jnp.where(low, outs[0], outs[1])
        return carry

    lax.fori_loop(0, seq // t, q_tile, 0)


def _attention(q_aug, k_aug, v):
    B, S, _ = q_aug.shape
    t = MOBA_BLOCK
    pair = lambda w: pl.BlockSpec((1, S, w), lambda b, p: (b, 0, p))
    return pl.pallas_call(
        functools.partial(_attention_kernel, seq=S, t=t),
        grid=(B, N_HEADS // 2),
        in_specs=[pair(2 * LANES), pair(2 * LANES), pair(LANES)],
        out_specs=pair(LANES),
        out_shape=jax.ShapeDtypeStruct((B, S, GROUP_WIDTH), F32),
        scratch_shapes=[pltpu.VMEM((S, LANES), BF16), pltpu.VMEM((S, LANES), BF16),
                        pltpu.VMEM((2, t, LANES), F32), pltpu.VMEM((2, t, LANES), F32)],
        compiler_params=pltpu.CompilerParams(dimension_semantics=("arbitrary", "arbitrary"),
                                             vmem_limit_bytes=VMEM_LIMIT),
        name="pair_attention",
    )(q_aug, k_aug, v)


def _mem_kv_kernel(mem_ref, g_ref, w_ref, kv_ref):
    kv_ref[0] = _dot(_rms(mem_ref[0], g_ref[...]).astype(BF16), w_ref[...]).astype(BF16)


def _memory_kv(mem, g_mem, w_kv):
    B, M, D = mem.shape
    return pl.pallas_call(
        _mem_kv_kernel,
        grid=(B,),
        in_specs=[pl.BlockSpec((1, M, D), lambda b: (b, 0, 0)), _const_spec((1, D)), _const_spec((D, 2 * D))],
        out_specs=pl.BlockSpec((1, M, 2 * D), lambda b: (b, 0, 0)),
        out_shape=jax.ShapeDtypeStruct((B, M, 2 * D), BF16),
        compiler_params=pltpu.CompilerParams(dimension_semantics=("arbitrary",), vmem_limit_bytes=VMEM_LIMIT),
        name="memory_kv",
    )(mem, g_mem.reshape(1, D), w_kv.astype(BF16))


def _post_kernel(of_ref, om_ref, x_ref, kv_ref, gfox_ref, gmoba_ref, gcross_ref, gffn_ref, gfin_ref,
                 wout_ref, wq_ref, wo_ref, wgu_ref, cw_ref, wd_ref, out_ref,
                 a3_ref, ffn_ref, gbuf_ref, halo_ref, *, tm, n_chunks, final_norm):
    i = pl.program_id(1)

    @pl.when(i == 0)
    def _():
        halo_ref[...] = jnp.zeros_like(halo_ref)

    yf = _rms(of_ref[0], gfox_ref[...]).astype(BF16)
    ym = _rms(om_ref[0], gmoba_ref[...]).astype(BF16)
    h = x_ref[0] + _dot(yf, wout_ref[0:GROUP_WIDTH, :]) + _dot(ym, wout_ref[GROUP_WIDTH:2 * GROUP_WIDTH, :])

    qc = (_dot(_rms(h, gcross_ref[...]).astype(BF16), wq_ref[...]) * (CROSS_HEAD_DIM ** -0.5)).astype(BF16)
    heads = []
    for hd in range(N_CROSS_HEADS):
        lo = hd * CROSS_HEAD_DIM
        s = _dot_nt(qc[:, lo:lo + CROSS_HEAD_DIM], kv_ref[0, :, lo:lo + CROSS_HEAD_DIM])
        p = jnp.exp(s - jnp.max(s, axis=1, keepdims=True))
        o = _dot(p.astype(BF16), kv_ref[0, :, D_MODEL + lo:D_MODEL + lo + CROSS_HEAD_DIM])
        heads.append((o / jnp.sum(p, axis=1, keepdims=True)).astype(BF16))
    h = h + _dot(jnp.concatenate(heads, axis=1), wo_ref[...])

    a3_ref[...] = _rms(h, gffn_ref[...]).astype(BF16)
    ffn_ref[...] = jnp.zeros_like(ffn_ref)

    def chunk(c, carry):
        gu = _dot(a3_ref[...], wgu_ref[c])
        g = gu[:, :FFN_CHUNK]
        gbuf_ref[0:SUBLANES, :] = halo_ref[c]
        gbuf_ref[SUBLANES:SUBLANES + tm, :] = g
        halo_ref[c] = g[tm - SUBLANES:tm, :]
        cw = cw_ref[c]
        gc = (cw[3:4] + cw[0:1] * gbuf_ref[pl.ds(SUBLANES - 2, tm), :]
              + cw[1:2] * gbuf_ref[pl.ds(SUBLANES - 1, tm), :] + cw[2:3] * g)
        act = (gc * jax.nn.sigmoid(gc) * gu[:, FFN_CHUNK:]).astype(BF16)
        ffn_ref[...] += _dot(act, wd_ref[c])
        return carry

    lax.fori_loop(0, n_chunks, chunk, 0)
    h = h + ffn_ref[...]
    out_ref[0] = _rms(h, gfin_ref[...]) if final_norm else h


def _post_attention(o_f, o_m, x, kv, g_fox, g_moba, w_out, g_cross, w_q, w_o, g_ffn, w_up, conv_w, conv_b,
                    w_down, g_final, final_norm):
    B, S, D = x.shape
    M = kv.shape[1]
    tm = MOBA_BLOCK
    d_ff = w_down.shape[0]
    assert d_ff % FFN_CHUNK == 0
    nc = d_ff // FFN_CHUNK
    w_g = w_up[:, :d_ff].reshape(D, nc, FFN_CHUNK)
    w_u = w_up[:, d_ff:].reshape(D, nc, FFN_CHUNK)
    w_gu = jnp.concatenate([w_g, w_u], axis=2).transpose(1, 0, 2).astype(BF16)
    w_d = w_down.reshape(nc, FFN_CHUNK, D).astype(BF16)
    cw = jnp.concatenate([conv_w, conv_b[None, :], jnp.zeros((SUBLANES - CONV_WIDTH - 1, d_ff), F32)], axis=0)
    cw = cw.reshape(SUBLANES, nc, FFN_CHUNK).transpose(1, 0, 2)
    row = lambda g: g.reshape(1, -1).astype(F32)
    tile = lambda w: pl.BlockSpec((1, tm, w), lambda b, i: (b, i, 0))
    return pl.pallas_call(
        functools.partial(_post_kernel, tm=tm, n_chunks=nc, final_norm=final_norm),
        grid=(B, S // tm),
        in_specs=[tile(GROUP_WIDTH), tile(GROUP_WIDTH), tile(D),
                  pl.BlockSpec((1, M, 2 * D), lambda b, i: (b, 0, 0)),
                  _const_spec((1, GROUP_WIDTH)), _const_spec((1, GROUP_WIDTH)), _const_spec((1, D)),
                  _const_spec((1, D)), _const_spec((1, D)),
                  _const_spec((2 * GROUP_WIDTH, D)), _const_spec((D, D)), _const_spec((D, D)),
                  _const_spec(w_gu.shape), _const_spec(cw.shape), _const_spec(w_d.shape)],
        out_specs=tile(D),
        out_shape=jax.ShapeDtypeStruct((B, S, D), F32),
        scratch_shapes=[pltpu.VMEM((tm, D), BF16), pltpu.VMEM((tm, D), F32),
                        pltpu.VMEM((tm + SUBLANES, FFN_CHUNK), F32),
                        pltpu.VMEM((nc, SUBLANES, FFN_CHUNK), F32)],
        compiler_params=pltpu.CompilerParams(dimension_semantics=("arbitrary", "arbitrary"),
                                             vmem_limit_bytes=VMEM_LIMIT),
        name="post_attention",
    )(o_f, o_m, x, kv, row(g_fox), row(g_moba), row(g_cross), row(g_ffn), row(g_final),
      w_out.astype(BF16), w_q.astype(BF16), w_o.astype(BF16), w_gu, cw, w_d)


def kernel(x, mem, g_mix, w_in, b_forget, g_fox, g_moba, w_out, g_cross, g_mem, w_q_mem, w_kv_mem,
           w_o_mem, g_ffn, w_up, conv_w, conv_b, w_down, g_final):
    depth = g_mix.shape[0]
    slopes = 2.0 ** (-8.0 * jnp.arange(1, N_HEADS + 1, dtype=F32) / N_HEADS)
    h = x
    for l in range(depth):
        qf, kf, vf, qm, km, vm = _in_projection(h, g_mix[l], w_in[l], b_forget[l], slopes)
        o_f = _attention(qf, kf, vf)
        o_m = _attention(qm, km, vm)
        kv = _memory_kv(mem, g_mem[l], w_kv_mem[l])
        h = _post_attention(o_f, o_m, h, kv, g_fox[l], g_moba[l], w_out[l], g_cross[l], w_q_mem[l],
                            w_o_mem[l], g_ffn[l], w_up[l], conv_w[l], conv_b[l], w_down[l], g_final,
                            final_norm=(l == depth - 1))
    return h
```

```python
import functools

import numpy as np
import jax
import jax.numpy as jnp
from jax import lax
from jax.experimental import pallas as pl
from jax.experimental.pallas import tpu as pltpu

F32 = jnp.float32
BF16 = jnp.bfloat16

D_MODEL = 1024
HEAD_DIM = 64
N_HEADS = 8
GROUP_WIDTH = N_HEADS * HEAD_DIM
MOBA_BLOCK = 256
MOBA_TOP_K = 3
MAX_MOBA_BLOCKS = 32
N_CROSS_HEADS = 4
CROSS_HEAD_DIM = D_MODEL // N_CROSS_HEADS
CONV_WIDTH = 3
EPS = 1e-6

LANES = 128
SUBLANES = 8
AUG = N_HEADS * LANES
FLOGIT_PAD = LANES
MASK_NEG = -2.0 ** 100
LOG2E = 1.4426950408889634
QK_SCALE = HEAD_DIM ** -0.5 * LOG2E
BIAS_LANE = 0
MASK_LANE = 8
FFN_CHUNK = 256
STEPS_PER_TRIP = 8
VMEM_LIMIT = 56 * 1024 * 1024


def _extras_base(h):
    return h * LANES + (HEAD_DIM if h % 2 == 0 else 0)


def _lane_tables(slopes):
    own = np.zeros((AUG,), np.float32)
    sign_q = np.zeros((AUG,), np.float32)
    sign_k = np.zeros((AUG,), np.float32)
    head_of = np.zeros((AUG,), np.int32)
    piece = np.zeros((AUG,), np.float32)
    ones_q = np.zeros((AUG,), np.float32)
    ones_k = np.zeros((AUG,), np.float32)
    blk = np.full((AUG,), -1.0, np.float32)
    for h in range(N_HEADS):
        head_of[h * LANES:(h + 1) * LANES] = h
        q0 = h * LANES + (0 if h % 2 == 0 else HEAD_DIM)
        own[q0:q0 + HEAD_DIM] = 1.0
        b = _extras_base(h) + BIAS_LANE
        for k in range(3):
            sign_q[b + k] = -1.0
            ones_k[b + k] = 1.0
            ones_q[b + 3 + k] = 1.0
            sign_k[b + 3 + k] = 1.0
            piece[b + k] = k
            piece[b + 3 + k] = k
        m = _extras_base(h) + MASK_LANE
        blk[m:m + MAX_MOBA_BLOCKS] = np.arange(MAX_MOBA_BLOCKS)
    slope_lane = slopes.astype(F32)[head_of] * LOG2E
    rows = [jnp.asarray(own), jnp.asarray(sign_q) * slope_lane, jnp.asarray(sign_k) * slope_lane,
            jnp.asarray(piece), jnp.asarray(ones_q), jnp.asarray(ones_k), jnp.asarray(blk),
            jnp.zeros((AUG,), F32)]
    return jnp.stack(rows, axis=0)


def _placement_matrices():
    p_fox = np.zeros((3 * LANES, 2 * AUG), np.float32)
    p_mask = np.zeros((N_HEADS * MAX_MOBA_BLOCKS, AUG), np.float32)
    for h in range(N_HEADS):
        b = _extras_base(h) + BIAS_LANE
        for k in range(3):
            p_fox[k * LANES + h, b + k] = 1.0
            p_fox[k * LANES + h, AUG + b + 3 + k] = -1.0
        m = _extras_base(h) + MASK_LANE
        for j in range(MAX_MOBA_BLOCKS):
            p_mask[h * MAX_MOBA_BLOCKS + j, m + j] = 1.0
    return jnp.asarray(p_fox, BF16), jnp.asarray(p_mask, BF16)


def _split3(v):
    hi = v.astype(BF16)
    r = v - hi.astype(F32)
    mid = r.astype(BF16)
    lo = (r - mid.astype(F32)).astype(BF16)
    return hi, mid, lo


def _split2(v):
    hi = v.astype(BF16)
    return hi, (v - hi.astype(F32)).astype(BF16)


def _rms(x, g):
    return x * lax.rsqrt(jnp.mean(x * x, axis=-1, keepdims=True) + EPS) * g


def _dot(a, b):
    return jnp.dot(a, b, preferred_element_type=F32)


def _dot_nt(a, b):
    return lax.dot_general(a, b, (((1,), (1,)), ((), ())), preferred_element_type=F32)


def _dup_pairs(t):
    cols = []
    for p in range(GROUP_WIDTH // LANES):
        c = t[:, p * LANES:(p + 1) * LANES]
        cols += [c, c]
    return jnp.concatenate(cols, axis=1)


def _const_spec(shape):
    return pl.BlockSpec(shape, lambda *_: (0,) * len(shape), pipeline_mode=pl.Buffered(1))


def _inproj_kernel(x_ref, g_ref, w_ref, bf_ref, tri_ref, pfox_ref, pmask_ref, meta_ref,
                   qf_ref, kf_ref, vf_ref, qm_ref, km_ref, vm_ref, carry_ref, kbdt_ref):
    tm = MOBA_BLOCK
    i = pl.program_id(1)

    @pl.when(i == 0)
    def _():
        carry_ref[...] = jnp.zeros_like(carry_ref)
        kbdt_ref[...] = jnp.zeros_like(kbdt_ref)

    a = _rms(x_ref[0], g_ref[...]).astype(BF16)

    def proj(c0, n):
        return _dot(a, w_ref[:, c0:c0 + n])

    meta = meta_ref[...]
    own = meta[0:1] > 0.5
    piece_idx = meta[3:4]
    ones_q = meta[4:5]
    ones_k = meta[5:6]

    z = proj(6 * GROUP_WIDTH, FLOGIT_PAD) + bf_ref[...]
    log_f = jnp.minimum(z, 0.0) - jnp.log1p(jnp.exp(-jnp.abs(z)))
    tri = tri_ref[...]
    f_hi, f_mid, f_lo = _split3(log_f)
    c = (_dot(tri, f_hi) + _dot(tri, f_mid) + _dot(tri, f_lo)) + carry_ref[...]
    carry_ref[...] = c[tm - 1:tm, :]
    ext = _dot(jnp.concatenate(_split3(c * LOG2E), axis=1), pfox_ref[...])
    qf = proj(0, GROUP_WIDTH) * QK_SCALE
    kf = proj(GROUP_WIDTH, GROUP_WIDTH)
    qf_ref[0] = jnp.where(own, _dup_pairs(qf), ext[:, :AUG] + ones_q).astype(BF16)
    kf_ref[0] = jnp.where(own, _dup_pairs(kf), ext[:, AUG:] + ones_k).astype(BF16)
    vf_ref[0] = proj(2 * GROUP_WIDTH, GROUP_WIDTH).astype(BF16)

    qm = proj(3 * GROUP_WIDTH, GROUP_WIDTH)
    km = proj(4 * GROUP_WIDTH, GROUP_WIDTH)
    vm_ref[0] = proj(5 * GROUP_WIDTH, GROUP_WIDTH).astype(BF16)

    kb_hi, kb_lo = _split2(kbdt_ref[...])
    q_hi, q_lo = _split2(qm)
    gate_t = _dot_nt(kb_hi, q_hi) + _dot_nt(kb_hi, q_lo) + _dot_nt(kb_lo, q_hi)

    jrow = lax.broadcasted_iota(jnp.int32, (MAX_MOBA_BLOCKS, tm), 0)
    past = jrow < i
    masks = []
    for h in range(N_HEADS):
        g = jnp.where(past, gate_t[h * MAX_MOBA_BLOCKS:(h + 1) * MAX_MOBA_BLOCKS], -jnp.inf)
        sel = jrow == i
        for _ in range(MOBA_TOP_K):
            top = jnp.max(g, axis=0, keepdims=True)
            cand = past & (g == top)
            first = jnp.min(jnp.where(cand, jrow, MAX_MOBA_BLOCKS), axis=0, keepdims=True)
            pick = jrow == first
            sel = sel | pick
            g = jnp.where(pick, -jnp.inf, g)
        masks.append(jnp.where(sel, 0.0, MASK_NEG))
    mask = jnp.concatenate(masks, axis=0).T.astype(BF16)
    ext_mask = _dot(mask, pmask_ref[...])

    pos = (i * tm + lax.broadcasted_iota(jnp.int32, (tm, 1), 0)).astype(F32)

    def bias_pieces(full):
        hi = full.astype(BF16).astype(F32)
        r = full - hi
        mid = r.astype(BF16).astype(F32)
        return jnp.where(piece_idx == 0.0, hi, jnp.where(piece_idx == 1.0, mid, r - mid))

    ext_q = ext_mask + bias_pieces(pos * meta[1:2]) + ones_q
    onehot = jnp.where(meta[6:7] == i.astype(F32), 1.0, 0.0)
    ext_k = bias_pieces(pos * meta[2:3]) + ones_k + onehot
    qm_ref[0] = jnp.where(own, _dup_pairs(qm * QK_SCALE), ext_q).astype(BF16)
    km_ref[0] = jnp.where(own, _dup_pairs(km), ext_k).astype(BF16)

    kmean = jnp.mean(km, axis=0, keepdims=True)
    head_of_lane = lax.broadcasted_iota(jnp.int32, (1, GROUP_WIDTH), 1) // HEAD_DIM
    for h in range(N_HEADS):
        kbdt_ref[pl.ds(h * MAX_MOBA_BLOCKS + i, 1), :] = jnp.where(head_of_lane == h, kmean, 0.0)


def _in_projection(x, g_mix, w_in, b_forget, slopes):
    B, S, D = x.shape
    tm = MOBA_BLOCK
    n_blk = S // tm
    assert S % tm == 0 and MOBA_TOP_K <= n_blk <= MAX_MOBA_BLOCKS and D == D_MODEL
    c = np.cumsum([GROUP_WIDTH, GROUP_WIDTH, GROUP_WIDTH, N_HEADS, GROUP_WIDTH, GROUP_WIDTH])
    w_f = jnp.pad(w_in[:, c[2]:c[3]], ((0, 0), (0, FLOGIT_PAD - N_HEADS)))
    w_r = jnp.concatenate([w_in[:, :c[2]], w_in[:, c[3]:], w_f], axis=1).astype(BF16)
    n_cols = w_r.shape[1]
    b_f = jnp.pad(b_forget.astype(F32), (0, FLOGIT_PAD - N_HEADS)).reshape(1, FLOGIT_PAD)
    tri = jnp.asarray(np.tril(np.ones((tm, tm), np.float32)), BF16)
    p_fox, p_mask = _placement_matrices()
    meta = _lane_tables(slopes)

    aug = jax.ShapeDtypeStruct((B, S, AUG), BF16)
    val = jax.ShapeDtypeStruct((B, S, GROUP_WIDTH), BF16)
    tile = lambda w: pl.BlockSpec((1, tm, w), lambda b, i: (b, i, 0))
    return pl.pallas_call(
        _inproj_kernel,
        grid=(B, n_blk),
        in_specs=[tile(D), _const_spec((1, D)), _const_spec((D, n_cols)), _const_spec((1, FLOGIT_PAD)),
                  _const_spec((tm, tm)), _const_spec(p_fox.shape), _const_spec(p_mask.shape),
                  _const_spec(meta.shape)],
        out_specs=[tile(AUG), tile(AUG), tile(GROUP_WIDTH), tile(AUG), tile(AUG), tile(GROUP_WIDTH)],
        out_shape=[aug, aug, val, aug, aug, val],
        scratch_shapes=[pltpu.VMEM((1, FLOGIT_PAD), F32),
                        pltpu.VMEM((N_HEADS * MAX_MOBA_BLOCKS, GROUP_WIDTH), F32)],
        compiler_params=pltpu.CompilerParams(dimension_semantics=("arbitrary", "arbitrary"),
                                             vmem_limit_bytes=VMEM_LIMIT),
        name="in_projection",
    )(x, g_mix.reshape(1, D), w_r, b_f, tri, p_fox, p_mask, meta)


def _attention_kernel(q_ref, k_ref, v_ref, o_ref, va_ref, vb_ref, s_ref, p_ref, alpha_ref, m_ref, acc_ref,
                      *, seq, t):
    nq = seq // t
    reps = t // LANES
    low = lax.broadcasted_iota(jnp.int32, (1, LANES), 1) < HEAD_DIM
    vp = v_ref[0]
    one = jnp.ones((), BF16)
    va_ref[...] = jnp.where(low, vp, one)
    vb_ref[...] = jnp.where(low, one, vp)
    v_refs = (va_ref, vb_ref)
    causal = (lax.broadcasted_iota(jnp.int32, (t, t), 1) <= lax.broadcasted_iota(jnp.int32, (t, t), 0))

    def rows(tile):
        return pl.ds(pl.multiple_of(tile * t, t), t)

    def stage_a(qi, kt, slot, diag):
        for hh in range(2):
            cols = slice(hh * LANES, (hh + 1) * LANES)
            s = _dot_nt(q_ref[0, rows(qi), cols], k_ref[0, rows(kt), cols])
            s_ref[slot, hh] = jnp.where(causal, s, -jnp.inf) if diag else s

    def stage_b(qi, slot, diag):
        for hh in range(2):
            s = s_ref[slot, hh]
            m_cur = jnp.max(s, axis=1, keepdims=True)
            if diag:
                m_next = jnp.broadcast_to(m_cur, (t, LANES))
            else:
                m_prev = m_ref[hh, rows(qi)]
                m_next = jnp.maximum(m_prev, m_cur)
                alpha_ref[slot, hh] = jnp.exp2(m_prev - m_next)
            p_ref[slot, hh] = jnp.exp2(s - jnp.concatenate([m_next] * reps, axis=1)).astype(BF16)
            m_ref[hh, rows(qi)] = m_next

    def stage_c(qi, kt, slot, diag):
        for hh in range(2):
            pv = _dot(p_ref[slot, hh], v_refs[hh][rows(kt), :])
            acc_ref[hh, rows(qi)] = pv if diag else alpha_ref[slot, hh] * acc_ref[hh, rows(qi)] + pv

    def run_pipeline(n_steps, first, advance, diag, unroll):
        def step(parity, cs):
            ca, cb, cc = cs
            stage_a(*ca, parity, diag)
            stage_b(cb[0], 1 - parity, diag)
            stage_c(*cc, parity, diag)
            return advance(*ca), advance(*cb), advance(*cc)

        def trip(_, cs):
            for u in range(unroll):
                cs = step(u % 2, cs)
            return cs

        c0 = first
        c1 = advance(*c0)
        stage_a(*c0, 0, diag)
        stage_a(*c1, 1, diag)
        stage_b(c0[0], 0, diag)
        cs = (advance(*c1), c1, c0)
        n_steady = n_steps - 2
        cs = lax.fori_loop(0, n_steady // unroll, trip, cs)
        f = 2 + unroll * (n_steady // unroll)
        for _ in range(n_steady % unroll):
            cs = step(f % 2, cs)
            f += 1
        _, cb, cc = cs
        stage_b(cb[0], (f - 1) % 2, diag)
        stage_c(*cc, f % 2, diag)
        stage_c(*advance(*cc), (f + 1) % 2, diag)

    def next_below_diagonal(qi, kt):
        wrap = kt + 1 >= qi
        return jnp.where(wrap, qi + 1, qi), jnp.where(wrap, 0, kt + 1)

    i32 = jnp.int32
    run_pipeline(nq, (i32(0), i32(0)), lambda qi, kt: (qi + 1, kt + 1), diag=True, unroll=2)
    run_pipeline(nq * (nq - 1) // 2, (i32(1), i32(0)), next_below_diagonal, diag=False, unroll=STEPS_PER_TRIP)

    def normalise(tile, carry):
        a0 = acc_ref[0, rows(tile)]
        a1 = acc_ref[1, rows(tile)]
        den = pltpu.roll(jnp.where(low, a1, a0), HEAD_DIM, 1)
        o_ref[0, rows(tile), :] = jnp.where(low, a0, a1) / den
        return carry

    lax.fori_loop(0, nq, normalise, 0)


def _attention(q_aug, k_aug, v):
    B, S, _ = q_aug.shape
    t = MOBA_BLOCK
    assert S % t == 0 and S // t >= 3
    pair = lambda w: pl.BlockSpec((1, S, w), lambda b, p: (b, 0, p))
    return pl.pallas_call(
        functools.partial(_attention_kernel, seq=S, t=t),
        grid=(B, N_HEADS // 2),
        in_specs=[pair(2 * LANES), pair(2 * LANES), pair(LANES)],
        out_specs=pair(LANES),
        out_shape=jax.ShapeDtypeStruct((B, S, GROUP_WIDTH), F32),
        scratch_shapes=[pltpu.VMEM((S, LANES), BF16), pltpu.VMEM((S, LANES), BF16),
                        pltpu.VMEM((2, 2, t, t), F32),
                        pltpu.VMEM((2, 2, t, t), BF16),
                        pltpu.VMEM((2, 2, t, LANES), F32),
                        pltpu.VMEM((2, S, LANES), F32),
                        pltpu.VMEM((2, S, LANES), F32)],
        compiler_params=pltpu.CompilerParams(dimension_semantics=("arbitrary", "arbitrary"),
                                             vmem_limit_bytes=VMEM_LIMIT),
        name="pair_attention",
    )(q_aug, k_aug, v)


def _mem_kv_kernel(mem_ref, g_ref, w_ref, kv_ref):
    kv_ref[0] = _dot(_rms(mem_ref[0], g_ref[...]).astype(BF16), w_ref[...]).astype(BF16)


def _memory_kv(mem, g_mem, w_kv):
    B, M, D = mem.shape
    return pl.pallas_call(
        _mem_kv_kernel,
        grid=(B,),
        in_specs=[pl.BlockSpec((1, M, D), lambda b: (b, 0, 0)), _const_spec((1, D)), _const_spec((D, 2 * D))],
        out_specs=pl.BlockSpec((1, M, 2 * D), lambda b: (b, 0, 0)),
        out_shape=jax.ShapeDtypeStruct((B, M, 2 * D), BF16),
        compiler_params=pltpu.CompilerParams(dimension_semantics=("arbitrary",), vmem_limit_bytes=VMEM_LIMIT),
        name="memory_kv",
    )(mem, g_mem.reshape(1, D), w_kv.astype(BF16))


def _post_kernel(of_ref, om_ref, x_ref, kv_ref, gfox_ref, gmoba_ref, gcross_ref, gffn_ref, gfin_ref,
                 wout_ref, wq_ref, wo_ref, wgu_ref, cw_ref, wd_ref, out_ref,
                 a3_ref, ffn_ref, gbuf_ref, halo_ref, *, tm, n_chunks, final_norm):
    i = pl.program_id(1)

    @pl.when(i == 0)
    def _():
        halo_ref[...] = jnp.zeros_like(halo_ref)

    yf = _rms(of_ref[0], gfox_ref[...]).astype(BF16)
    ym = _rms(om_ref[0], gmoba_ref[...]).astype(BF16)
    h = x_ref[0] + _dot(yf, wout_ref[0:GROUP_WIDTH, :]) + _dot(ym, wout_ref[GROUP_WIDTH:2 * GROUP_WIDTH, :])

    qc = (_dot(_rms(h, gcross_ref[...]).astype(BF16), wq_ref[...]) * (CROSS_HEAD_DIM ** -0.5)).astype(BF16)
    heads = []
    for hd in range(N_CROSS_HEADS):
        lo = hd * CROSS_HEAD_DIM
        s = _dot_nt(qc[:, lo:lo + CROSS_HEAD_DIM], kv_ref[0, :, lo:lo + CROSS_HEAD_DIM])
        p = jnp.exp(s - jnp.max(s, axis=1, keepdims=True))
        o = _dot(p.astype(BF16), kv_ref[0, :, D_MODEL + lo:D_MODEL + lo + CROSS_HEAD_DIM])
        heads.append((o / jnp.sum(p, axis=1, keepdims=True)).astype(BF16))
    h = h + _dot(jnp.concatenate(heads, axis=1), wo_ref[...])

    a3_ref[...] = _rms(h, gffn_ref[...]).astype(BF16)
    ffn_ref[...] = jnp.zeros_like(ffn_ref)

    def chunk(c, carry):
        gu = _dot(a3_ref[...], wgu_ref[c])
        g = gu[:, :FFN_CHUNK]
        gbuf_ref[0:SUBLANES, :] = halo_ref[c]
        gbuf_ref[SUBLANES:SUBLANES + tm, :] = g
        halo_ref[c] = g[tm - SUBLANES:tm, :]
        cw = cw_ref[c]
        gc = (cw[3:4] + cw[0:1] * gbuf_ref[pl.ds(SUBLANES - 2, tm), :]
              + cw[1:2] * gbuf_ref[pl.ds(SUBLANES - 1, tm), :] + cw[2:3] * g)
        act = (gc * jax.nn.sigmoid(gc) * gu[:, FFN_CHUNK:]).astype(BF16)
        ffn_ref[...] += _dot(act, wd_ref[c])
        return carry

    lax.fori_loop(0, n_chunks, chunk, 0)
    h = h + ffn_ref[...]
    out_ref[0] = _rms(h, gfin_ref[...]) if final_norm else h


def _post_attention(o_f, o_m, x, kv, g_fox, g_moba, w_out, g_cross, w_q, w_o, g_ffn, w_up, conv_w, conv_b,
                    w_down, g_final, final_norm):
    B, S, D = x.shape
    M = kv.shape[1]
    tm = MOBA_BLOCK
    d_ff = w_down.shape[0]
    assert d_ff % FFN_CHUNK == 0
    nc = d_ff // FFN_CHUNK
    w_g = w_up[:, :d_ff].reshape(D, nc, FFN_CHUNK)
    w_u = w_up[:, d_ff:].reshape(D, nc, FFN_CHUNK)
    w_gu = jnp.concatenate([w_g, w_u], axis=2).transpose(1, 0, 2).astype(BF16)
    w_d = w_down.reshape(nc, FFN_CHUNK, D).astype(BF16)
    cw = jnp.concatenate([conv_w, conv_b[None, :], jnp.zeros((SUBLANES - CONV_WIDTH - 1, d_ff), F32)], axis=0)
    cw = cw.reshape(SUBLANES, nc, FFN_CHUNK).transpose(1, 0, 2)
    row = lambda g: g.reshape(1, -1).astype(F32)
    tile = lambda w: pl.BlockSpec((1, tm, w), lambda b, i: (b, i, 0))
    return pl.pallas_call(
        functools.partial(_post_kernel, tm=tm, n_chunks=nc, final_norm=final_norm),
        grid=(B, S // tm),
        in_specs=[tile(GROUP_WIDTH), tile(GROUP_WIDTH), tile(D),
                  pl.BlockSpec((1, M, 2 * D), lambda b, i: (b, 0, 0)),
                  _const_spec((1, GROUP_WIDTH)), _const_spec((1, GROUP_WIDTH)), _const_spec((1, D)),
                  _const_spec((1, D)), _const_spec((1, D)),
                  _const_spec((2 * GROUP_WIDTH, D)), _const_spec((D, D)), _const_spec((D, D)),
                  _const_spec(w_gu.shape), _const_spec(cw.shape), _const_spec(w_d.shape)],
        out_specs=tile(D),
        out_shape=jax.ShapeDtypeStruct((B, S, D), F32),
        scratch_shapes=[pltpu.VMEM((tm, D), BF16), pltpu.VMEM((tm, D), F32),
                        pltpu.VMEM((tm + SUBLANES, FFN_CHUNK), F32),
                        pltpu.VMEM((nc, SUBLANES, FFN_CHUNK), F32)],
        compiler_params=pltpu.CompilerParams(dimension_semantics=("arbitrary", "arbitrary"),
                                             vmem_limit_bytes=VMEM_LIMIT),
        name="post_attention",
    )(o_f, o_m, x, kv, row(g_fox), row(g_moba), row(g_cross), row(g_ffn), row(g_final),
      w_out.astype(BF16), w_q.astype(BF16), w_o.astype(BF16), w_gu, cw, w_d)


def kernel(x, mem, g_mix, w_in, b_forget, g_fox, g_moba, w_out, g_cross, g_mem, w_q_mem, w_kv_mem,
           w_o_mem, g_ffn, w_up, conv_w, conv_b, w_down, g_final):
    depth = g_mix.shape[0]
    slopes = 2.0 ** (-8.0 * jnp.arange(1, N_HEADS + 1, dtype=F32) / N_HEADS)
    h = x
    for l in range(depth):
        qf, kf, vf, qm, km, vm = _in_projection(h, g_mix[l], w_in[l], b_forget[l], slopes)
        o_f = _attention(qf, kf, vf)
        o_m = _attention(qm, km, vm)
        kv = _memory_kv(mem, g_mem[l], w_kv_mem[l])
        h = _post_attention(o_f, o_m, h, kv, g_fox[l], g_moba[l], w_out[l], g_cross[l], w_q_mem[l],
                            w_o_mem[l], g_ffn[l], w_up[l], conv_w[l], conv_b[l], w_down[l], g_final,
                            final_norm=(l == depth - 1))
    return h
```

```python
import functools

import numpy as np
import jax
import jax.numpy as jnp
from jax import lax
from jax.experimental import pallas as pl
from jax.experimental.pallas import tpu as pltpu

F32 = jnp.float32
BF16 = jnp.bfloat16

D_MODEL = 1024
HEAD_DIM = 64
N_HEADS = 8
GROUP_WIDTH = N_HEADS * HEAD_DIM
MOBA_BLOCK = 256
MOBA_TOP_K = 3
MAX_MOBA_BLOCKS = 32
N_CROSS_HEADS = 4
CROSS_HEAD_DIM = D_MODEL // N_CROSS_HEADS
CONV_WIDTH = 3
EPS = 1e-6

LANES = 128
SUBLANES = 8
AUG = N_HEADS * LANES
FLOGIT_PAD = LANES
MASK_NEG = -2.0 ** 100
LOG2E = 1.4426950408889634
QK_SCALE = HEAD_DIM ** -0.5 * LOG2E
BIAS_LANE = 0
MASK_LANE = 8
FFN_CHUNK = 256
ATTN_TILE = 256
STEPS_PER_TRIP = 16
POST_TILE = 512
IN_BLOCKS_PER_STEP = 4
VMEM_LIMIT = 56 * 1024 * 1024


def _extras_base(h):
    return h * LANES + (HEAD_DIM if h % 2 == 0 else 0)


def _lane_tables(slopes):
    table = np.zeros((SUBLANES, AUG), np.float32)
    table[3] = -1.0
    head_of = np.repeat(np.arange(N_HEADS), LANES)
    for h in range(N_HEADS):
        q0 = h * LANES + (0 if h % 2 == 0 else HEAD_DIM)
        table[0, q0:q0 + HEAD_DIM] = 1.0
        b = _extras_base(h) + BIAS_LANE
        table[1, b + 3:b + 6] = 1.0
        table[2, b:b + 3] = 1.0
        table[4, b:b + 3] = -1.0
        table[5, b + 3:b + 6] = 1.0
        table[6, b:b + 6] = [0, 1, 2, 0, 1, 2]
        m = _extras_base(h) + MASK_LANE
        table[3, m:m + MAX_MOBA_BLOCKS] = np.arange(MAX_MOBA_BLOCKS)
    slope_of_lane = slopes.astype(F32)[head_of] * LOG2E
    scale = jnp.ones((SUBLANES, AUG), F32).at[4].set(slope_of_lane).at[5].set(slope_of_lane)
    return jnp.asarray(table) * scale


def _placement_matrices():
    p_fox = np.zeros((LANES, 2 * AUG), np.float32)
    p_mask = np.zeros((N_HEADS * MAX_MOBA_BLOCKS, AUG), np.float32)
    for h in range(N_HEADS):
        b = _extras_base(h) + BIAS_LANE
        for k in range(3):
            p_fox[k * N_HEADS + h, b + k] = 1.0
            p_fox[k * N_HEADS + h, AUG + b + 3 + k] = -1.0
        m = _extras_base(h) + MASK_LANE
        for j in range(MAX_MOBA_BLOCKS):
            p_mask[h * MAX_MOBA_BLOCKS + j, m + j] = 1.0
    return jnp.asarray(p_fox, BF16), jnp.asarray(p_mask, BF16)


def _split3(v):
    hi = v.astype(BF16)
    r = v - hi.astype(F32)
    mid = r.astype(BF16)
    lo = (r - mid.astype(F32)).astype(BF16)
    return hi, mid, lo


def _split2(v):
    hi = v.astype(BF16)
    return hi, (v - hi.astype(F32)).astype(BF16)


def _rms(x, g):
    return x * lax.rsqrt(jnp.mean(x * x, axis=-1, keepdims=True) + EPS) * g


def _dot(a, b):
    return jnp.dot(a, b, preferred_element_type=F32)


def _dot_nt(a, b):
    return lax.dot_general(a, b, (((1,), (1,)), ((), ())), preferred_element_type=F32)


def _dup_pairs(t):
    cols = []
    for p in range(GROUP_WIDTH // LANES):
        c = t[:, p * LANES:(p + 1) * LANES]
        cols += [c, c]
    return jnp.concatenate(cols, axis=1)


def _const_spec(shape):
    return pl.BlockSpec(shape, lambda *_: (0,) * len(shape), pipeline_mode=pl.Buffered(1))


def _inproj_kernel(x_ref, g_ref, w_ref, bf_ref, tri_ref, pfox_ref, pmask_ref, meta_ref,
                   qf_ref, kf_ref, vf_ref, qm_ref, km_ref, vm_ref, carry_ref, kbdt_ref, *, blocks_per_step):
    tm = MOBA_BLOCK
    i = pl.program_id(1)

    @pl.when(i == 0)
    def _():
        carry_ref[...] = jnp.zeros_like(carry_ref)
        kbdt_ref[...] = jnp.zeros_like(kbdt_ref)

    meta = meta_ref[...]
    own = meta[0:1] > 0.5
    ones_q = meta[1:2]
    ones_k = meta[2:3]
    block_of_lane = meta[3:4]
    piece_of_lane = meta[6:7]
    head_lane = lax.broadcasted_iota(jnp.int32, (1, FLOGIT_PAD), 1) < N_HEADS
    tri = tri_ref[...]
    jrow = lax.broadcasted_iota(jnp.int32, (MAX_MOBA_BLOCKS, tm), 0)
    head_of_lane = lax.broadcasted_iota(jnp.int32, (1, GROUP_WIDTH), 1) // HEAD_DIM

    def project(rows):
        a = _rms(x_ref[0, rows, :], g_ref[...]).astype(BF16)
        cols = [_dot(a, w_ref[:, c * GROUP_WIDTH:(c + 1) * GROUP_WIDTH]) for c in range(6)]
        return cols + [_dot(a, w_ref[:, 6 * GROUP_WIDTH:6 * GROUP_WIDTH + FLOGIT_PAD])]

    def finish(rows, blk, projected):
        qf, kf, vf, qm, km, vm, z = projected

        z = z + bf_ref[...]
        log_f = jnp.minimum(z, 0.0) - jnp.log1p(jnp.exp(-jnp.abs(z)))
        f_hi, f_mid, f_lo = _split3(log_f)
        c = (_dot(tri, f_hi) + _dot(tri, f_mid) + _dot(tri, f_lo)) + carry_ref[...]
        carry_ref[...] = c[tm - 1:tm, :]
        c_hi, c_mid, c_lo = _split3(jnp.where(head_lane, c * LOG2E, 0.0))
        packed = (c_hi.astype(F32) + pltpu.roll(c_mid.astype(F32), N_HEADS, 1)
                  + pltpu.roll(c_lo.astype(F32), 2 * N_HEADS, 1)).astype(BF16)
        ext = _dot(packed, pfox_ref[...])
        qf_ref[0, rows, :] = jnp.where(own, _dup_pairs(qf * QK_SCALE), ext[:, :AUG] + ones_q).astype(BF16)
        kf_ref[0, rows, :] = jnp.where(own, _dup_pairs(kf), ext[:, AUG:] + ones_k).astype(BF16)
        vf_ref[0, rows, :] = vf.astype(BF16)

        vm_ref[0, rows, :] = vm.astype(BF16)
        kb_hi, kb_lo = _split2(kbdt_ref[...])
        q_hi, q_lo = _split2(qm)
        gate_t = _dot_nt(kb_hi, q_hi) + _dot_nt(kb_hi, q_lo) + _dot_nt(kb_lo, q_hi)
        past = jrow < blk
        masks = []
        for h in range(N_HEADS):
            g = jnp.where(past, gate_t[h * MAX_MOBA_BLOCKS:(h + 1) * MAX_MOBA_BLOCKS], -jnp.inf)
            sel = jrow == blk
            for _ in range(MOBA_TOP_K):
                top = jnp.max(g, axis=0, keepdims=True)
                cand = past & (g == top)
                first = jnp.min(jnp.where(cand, jrow, MAX_MOBA_BLOCKS), axis=0, keepdims=True)
                pick = jrow == first
                sel = sel | pick
                g = jnp.where(pick, -jnp.inf, g)
            masks.append(jnp.where(sel, 0.0, MASK_NEG))
        mask = jnp.concatenate(masks, axis=0).T.astype(BF16)
        ext_mask = _dot(mask, pmask_ref[...])

        pos = (blk * tm + lax.broadcasted_iota(jnp.int32, (tm, 1), 0)).astype(F32)

        def bias_pieces(full):
            hi = full.astype(BF16).astype(F32)
            r = full - hi
            mid = r.astype(BF16).astype(F32)
            return jnp.where(piece_of_lane == 0.0, hi, jnp.where(piece_of_lane == 1.0, mid, r - mid))

        onehot = jnp.where(block_of_lane == blk.astype(F32), 1.0, 0.0)
        qm_ref[0, rows, :] = jnp.where(own, _dup_pairs(qm * QK_SCALE),
                                       ext_mask + bias_pieces(pos * meta[4:5]) + ones_q).astype(BF16)
        km_ref[0, rows, :] = jnp.where(own, _dup_pairs(km),
                                       bias_pieces(pos * meta[5:6]) + ones_k + onehot).astype(BF16)

        kmean = jnp.mean(km, axis=0, keepdims=True)
        for h in range(N_HEADS):
            kbdt_ref[pl.ds(h * MAX_MOBA_BLOCKS + blk, 1), :] = jnp.where(head_of_lane == h, kmean, 0.0)

    row_groups = [pl.ds(g * tm, tm) for g in range(blocks_per_step)]
    ahead = project(row_groups[0])
    for g in range(blocks_per_step):
        current = ahead
        if g + 1 < blocks_per_step:
            ahead = project(row_groups[g + 1])
        finish(row_groups[g], i * blocks_per_step + g, current)


def _in_projection(x, g_mix, w_in, b_forget, slopes):
    B, S, D = x.shape
    n_blk = S // MOBA_BLOCK
    bps = IN_BLOCKS_PER_STEP if n_blk % IN_BLOCKS_PER_STEP == 0 else 1
    tm = bps * MOBA_BLOCK
    assert S % MOBA_BLOCK == 0 and MOBA_TOP_K <= n_blk <= MAX_MOBA_BLOCKS and D == D_MODEL
    c = np.cumsum([GROUP_WIDTH, GROUP_WIDTH, GROUP_WIDTH, N_HEADS, GROUP_WIDTH, GROUP_WIDTH])
    w_f = jnp.pad(w_in[:, c[2]:c[3]], ((0, 0), (0, FLOGIT_PAD - N_HEADS)))
    w_r = jnp.concatenate([w_in[:, :c[2]], w_in[:, c[3]:], w_f], axis=1).astype(BF16)
    n_cols = w_r.shape[1]
    b_f = jnp.pad(b_forget.astype(F32), (0, FLOGIT_PAD - N_HEADS)).reshape(1, FLOGIT_PAD)
    tri = jnp.asarray(np.tril(np.ones((MOBA_BLOCK, MOBA_BLOCK), np.float32)), BF16)
    p_fox, p_mask = _placement_matrices()
    meta = _lane_tables(slopes)

    aug = jax.ShapeDtypeStruct((B, S, AUG), BF16)
    val = jax.ShapeDtypeStruct((B, S, GROUP_WIDTH), BF16)
    tile = lambda w: pl.BlockSpec((1, tm, w), lambda b, i: (b, i, 0))
    return pl.pallas_call(
        functools.partial(_inproj_kernel, blocks_per_step=bps),
        grid=(B, n_blk // bps),
        in_specs=[tile(D), _const_spec((1, D)), _const_spec((D, n_cols)), _const_spec((1, FLOGIT_PAD)),
                  _const_spec(tri.shape), _const_spec(p_fox.shape),
                  _const_spec(p_mask.shape), _const_spec(meta.shape)],
        out_specs=[tile(AUG), tile(AUG), tile(GROUP_WIDTH), tile(AUG), tile(AUG), tile(GROUP_WIDTH)],
        out_shape=[aug, aug, val, aug, aug, val],
        scratch_shapes=[pltpu.VMEM((1, FLOGIT_PAD), F32),
                        pltpu.VMEM((N_HEADS * MAX_MOBA_BLOCKS, GROUP_WIDTH), F32)],
        compiler_params=pltpu.CompilerParams(dimension_semantics=("arbitrary", "arbitrary"),
                                             vmem_limit_bytes=VMEM_LIMIT),
        name="in_projection",
    )(x, g_mix.reshape(1, D), w_r, b_f, tri, p_fox, p_mask, meta)


def _attention_kernel(q_ref, k_ref, v_ref, o_ref, va_ref, vb_ref, s_ref, p_ref, alpha_ref, m_ref, acc_ref,
                      *, seq, t):
    nq = seq // t
    reps = t // LANES
    low = lax.broadcasted_iota(jnp.int32, (1, LANES), 1) < HEAD_DIM
    vp = v_ref[0]
    one = jnp.ones((), BF16)
    va_ref[...] = jnp.where(low, vp, one)
    vb_ref[...] = jnp.where(low, one, vp)
    v_refs = (va_ref, vb_ref)
    causal = (lax.broadcasted_iota(jnp.int32, (t, t), 1) <= lax.broadcasted_iota(jnp.int32, (t, t), 0))

    def rows(tile):
        return pl.ds(pl.multiple_of(tile * t, t), t)

    def stage_a(qi, kt, slot, diag):
        for hh in range(2):
            cols = slice(hh * LANES, (hh + 1) * LANES)
            s = _dot_nt(q_ref[0, rows(qi), cols], k_ref[0, rows(kt), cols])
            s_ref[slot, hh] = jnp.where(causal, s, -jnp.inf) if diag else s

    def stage_b(qi, slot, diag):
        for hh in range(2):
            s = s_ref[slot, hh]
            m_cur = jnp.max(s, axis=1, keepdims=True)
            if diag:
                m_next = jnp.broadcast_to(m_cur, (t, LANES))
            else:
                m_prev = m_ref[hh, rows(qi)]
                m_next = jnp.maximum(m_prev, m_cur)
                alpha_ref[slot, hh] = jnp.exp2(m_prev - m_next)
            p_ref[slot, hh] = jnp.exp2(s - jnp.concatenate([m_next] * reps, axis=1)).astype(BF16)
            m_ref[hh, rows(qi)] = m_next

    def stage_c(qi, kt, slot, diag):
        for hh in range(2):
            pv = _dot(p_ref[slot, hh], v_refs[hh][rows(kt), :])
            acc_ref[hh, rows(qi)] = pv if diag else alpha_ref[slot, hh] * acc_ref[hh, rows(qi)] + pv

    def run_pipeline(n_steps, first, advance, diag, unroll):
        def step(parity, cs):
            ca, cb, cc = cs
            stage_a(*ca, parity, diag)
            stage_b(cb[0], 1 - parity, diag)
            stage_c(*cc, parity, diag)
            return advance(*ca), advance(*cb), advance(*cc)

        def trip(_, cs):
            for u in range(unroll):
                cs = step(u % 2, cs)
            return cs

        c0 = first
        c1 = advance(*c0)
        stage_a(*c0, 0, diag)
        stage_a(*c1, 1, diag)
        stage_b(c0[0], 0, diag)
        cs = (advance(*c1), c1, c0)
        n_steady = n_steps - 2
        cs = lax.fori_loop(0, n_steady // unroll, trip, cs)
        f = 2 + unroll * (n_steady // unroll)
        for _ in range(n_steady % unroll):
            cs = step(f % 2, cs)
            f += 1
        _, cb, cc = cs
        stage_b(cb[0], (f - 1) % 2, diag)
        stage_c(*cc, f % 2, diag)
        stage_c(*advance(*cc), (f + 1) % 2, diag)

    def next_below_diagonal(qi, kt):
        wrap = kt + 1 >= qi
        return jnp.where(wrap, qi + 1, qi), jnp.where(wrap, 0, kt + 1)

    i32 = jnp.int32
    run_pipeline(nq, (i32(0), i32(0)), lambda qi, kt: (qi + 1, kt + 1), diag=True, unroll=2)
    run_pipeline(nq * (nq - 1) // 2, (i32(1), i32(0)), next_below_diagonal, diag=False, unroll=STEPS_PER_TRIP)

    def normalise(tile, carry):
        a0 = acc_ref[0, rows(tile)]
        a1 = acc_ref[1, rows(tile)]
        den = pltpu.roll(jnp.where(low, a1, a0), HEAD_DIM, 1)
        o_ref[0, rows(tile), :] = jnp.where(low, a0, a1) / den
        return carry

    lax.fori_loop(0, nq, normalise, 0)


def _attention(q_aug, k_aug, v):
    B, S, _ = q_aug.shape
    t = ATTN_TILE if S % ATTN_TILE == 0 and S // ATTN_TILE >= 3 else MOBA_BLOCK
    assert S % t == 0 and S // t >= 3
    pair = lambda w, **kw: pl.BlockSpec((1, S, w), lambda b, p: (b, 0, p), **kw)
    single = dict(pipeline_mode=pl.Buffered(1))
    return pl.pallas_call(
        functools.partial(_attention_kernel, seq=S, t=t),
        grid=(B, N_HEADS // 2),
        in_specs=[pair(2 * LANES), pair(2 * LANES), pair(LANES, **single)],
        out_specs=pair(LANES, **single),
        out_shape=jax.ShapeDtypeStruct((B, S, GROUP_WIDTH), F32),
        scratch_shapes=[pltpu.VMEM((S, LANES), BF16), pltpu.VMEM((S, LANES), BF16),
                        pltpu.VMEM((2, 2, t, t), F32),
                        pltpu.VMEM((2, 2, t, t), BF16),
                        pltpu.VMEM((2, 2, t, LANES), F32),
                        pltpu.VMEM((2, S, LANES), F32),
                        pltpu.VMEM((2, S, LANES), F32)],
        compiler_params=pltpu.CompilerParams(dimension_semantics=("arbitrary", "arbitrary"),
                                             vmem_limit_bytes=VMEM_LIMIT),
        name="pair_attention",
    )(q_aug, k_aug, v)


def _mem_kv_kernel(mem_ref, g_ref, w_ref, kv_ref):
    kv_ref[0] = _dot(_rms(mem_ref[0], g_ref[...]).astype(BF16), w_ref[...]).astype(BF16)


def _memory_kv(mem, g_mem, w_kv):
    B, M, D = mem.shape
    return pl.pallas_call(
        _mem_kv_kernel,
        grid=(B,),
        in_specs=[pl.BlockSpec((1, M, D), lambda b: (b, 0, 0)), _const_spec((1, D)), _const_spec((D, 2 * D))],
        out_specs=pl.BlockSpec((1, M, 2 * D), lambda b: (b, 0, 0)),
        out_shape=jax.ShapeDtypeStruct((B, M, 2 * D), BF16),
        compiler_params=pltpu.CompilerParams(dimension_semantics=("arbitrary",), vmem_limit_bytes=VMEM_LIMIT),
        name="memory_kv",
    )(mem, g_mem.reshape(1, D), w_kv.astype(BF16))


def _post_kernel(of_ref, om_ref, x_ref, kv_ref, gfox_ref, gmoba_ref, gcross_ref, gffn_ref, gfin_ref,
                 wout_ref, wq_ref, wo_ref, wgu_ref, cw_ref, wd_ref, out_ref,
                 halo_ref, *, tm, n_chunks, final_norm):
    i = pl.program_id(1)

    @pl.when(i == 0)
    def _():
        halo_ref[...] = jnp.zeros_like(halo_ref)

    yf = _rms(of_ref[0], gfox_ref[...]).astype(BF16)
    ym = _rms(om_ref[0], gmoba_ref[...]).astype(BF16)
    h = x_ref[0] + _dot(yf, wout_ref[0:GROUP_WIDTH, :]) + _dot(ym, wout_ref[GROUP_WIDTH:2 * GROUP_WIDTH, :])

    qc = (_dot(_rms(h, gcross_ref[...]).astype(BF16), wq_ref[...]) * (CROSS_HEAD_DIM ** -0.5)).astype(BF16)
    heads = []
    for hd in range(N_CROSS_HEADS):
        lo = hd * CROSS_HEAD_DIM
        s = _dot_nt(qc[:, lo:lo + CROSS_HEAD_DIM], kv_ref[0, :, lo:lo + CROSS_HEAD_DIM])
        p = jnp.exp(s - jnp.max(s, axis=1, keepdims=True))
        o = _dot(p.astype(BF16), kv_ref[0, :, D_MODEL + lo:D_MODEL + lo + CROSS_HEAD_DIM])
        heads.append((o / jnp.sum(p, axis=1, keepdims=True)).astype(BF16))
    h = h + _dot(jnp.concatenate(heads, axis=1), wo_ref[...])

    a3 = _rms(h, gffn_ref[...]).astype(BF16)
    first_rows = lax.broadcasted_iota(jnp.int32, (SUBLANES, FFN_CHUNK), 0)

    def delayed(g, prev_tail, d):
        rolled = pltpu.roll(g, d, 0)
        head = jnp.where(first_rows < d, pltpu.roll(prev_tail, d, 0), rolled[:SUBLANES])
        return jnp.concatenate([head, rolled[SUBLANES:]], axis=0)

    gu_next = _dot(a3, wgu_ref[0])
    for c in range(n_chunks):
        gu = gu_next
        if c + 1 < n_chunks:
            gu_next = _dot(a3, wgu_ref[c + 1])
        g = gu[:, :FFN_CHUNK]
        prev_tail = halo_ref[c]
        halo_ref[c] = g[tm - SUBLANES:tm, :]
        cw = cw_ref[c]
        gc = cw[3:4] + cw[0:1] * delayed(g, prev_tail, 2) + cw[1:2] * delayed(g, prev_tail, 1) + cw[2:3] * g
        half = 0.5 * gc
        act = (half * (1.0 + jnp.tanh(half)) * gu[:, FFN_CHUNK:]).astype(BF16)
        h = h + _dot(act, wd_ref[c])
    out_ref[0] = _rms(h, gfin_ref[...]) if final_norm else h


def _post_attention(o_f, o_m, x, kv, g_fox, g_moba, w_out, g_cross, w_q, w_o, g_ffn, w_up, conv_w, conv_b,
                    w_down, g_final, final_norm):
    B, S, D = x.shape
    M = kv.shape[1]
    tm = POST_TILE
    d_ff = w_down.shape[0]
    assert d_ff % FFN_CHUNK == 0 and S % tm == 0
    nc = d_ff // FFN_CHUNK
    w_g = w_up[:, :d_ff].reshape(D, nc, FFN_CHUNK)
    w_u = w_up[:, d_ff:].reshape(D, nc, FFN_CHUNK)
    w_gu = jnp.concatenate([w_g, w_u], axis=2).transpose(1, 0, 2).astype(BF16)
    w_d = w_down.reshape(nc, FFN_CHUNK, D).astype(BF16)
    cw = jnp.concatenate([conv_w, conv_b[None, :], jnp.zeros((SUBLANES - CONV_WIDTH - 1, d_ff), F32)], axis=0)
    cw = cw.reshape(SUBLANES, nc, FFN_CHUNK).transpose(1, 0, 2)
    row = lambda g: g.reshape(1, -1).astype(F32)
    tile = lambda w: pl.BlockSpec((1, tm, w), lambda b, i: (b, i, 0))
    return pl.pallas_call(
        functools.partial(_post_kernel, tm=tm, n_chunks=nc, final_norm=final_norm),
        grid=(B, S // tm),
        in_specs=[tile(GROUP_WIDTH), tile(GROUP_WIDTH), tile(D),
                  pl.BlockSpec((1, M, 2 * D), lambda b, i: (b, 0, 0)),
                  _const_spec((1, GROUP_WIDTH)), _const_spec((1, GROUP_WIDTH)), _const_spec((1, D)),
                  _const_spec((1, D)), _const_spec((1, D)),
                  _const_spec((2 * GROUP_WIDTH, D)), _const_spec((D, D)), _const_spec((D, D)),
                  _const_spec(w_gu.shape), _const_spec(cw.shape), _const_spec(w_d.shape)],
        out_specs=tile(D),
        out_shape=jax.ShapeDtypeStruct((B, S, D), F32),
        scratch_shapes=[pltpu.VMEM((nc, SUBLANES, FFN_CHUNK), F32)],
        compiler_params=pltpu.CompilerParams(dimension_semantics=("arbitrary", "arbitrary"),
                                             vmem_limit_bytes=VMEM_LIMIT),
        name="post_attention",
    )(o_f, o_m, x, kv, row(g_fox), row(g_moba), row(g_cross), row(g_ffn), row(g_final),
      w_out.astype(BF16), w_q.astype(BF16), w_o.astype(BF16), w_gu, cw, w_d)


def kernel(x, mem, g_mix, w_in, b_forget, g_fox, g_moba, w_out, g_cross, g_mem, w_q_mem, w_kv_mem,
           w_o_mem, g_ffn, w_up, conv_w, conv_b, w_down, g_final):
    depth = g_mix.shape[0]
    slopes = 2.0 ** (-8.0 * jnp.arange(1, N_HEADS + 1, dtype=F32) / N_HEADS)
    h = x
    for l in range(depth):
        qf, kf, vf, qm, km, vm = _in_projection(h, g_mix[l], w_in[l], b_forget[l], slopes)
        o_f = _attention(qf, kf, vf)
        o_m = _attention(qm, km, vm)
        kv = _memory_kv(mem, g_mem[l], w_kv_mem[l])
        h = _post_attention(o_f, o_m, h, kv, g_fox[l], g_moba[l], w_out[l], g_cross[l], w_q_mem[l],
                            w_o_mem[l], g_ffn[l], w_up[l], conv_w[l], conv_b[l], w_down[l], g_final,
                            final_norm=(l == depth - 1))
    return h
```

```python
import functools

import numpy as np
import jax
import jax.numpy as jnp
from jax import lax
from jax.experimental import pallas as pl
from jax.experimental.pallas import tpu as pltpu

F32 = jnp.float32
BF16 = jnp.bfloat16

D_MODEL = 1024
HEAD_DIM = 64
N_HEADS = 8
GROUP_WIDTH = N_HEADS * HEAD_DIM
MOBA_BLOCK = 256
MOBA_TOP_K = 3
MAX_MOBA_BLOCKS = 32
N_CROSS_HEADS = 4
CROSS_HEAD_DIM = D_MODEL // N_CROSS_HEADS
CONV_WIDTH = 3
EPS = 1e-6

LANES = 128
SUBLANES = 8
AUG = N_HEADS * LANES
FLOGIT_PAD = LANES
MASK_NEG = -2.0 ** 100
LOG2E = 1.4426950408889634
QK_SCALE = HEAD_DIM ** -0.5 * LOG2E
BIAS_LANE = 0
MASK_LANE = 8
FFN_CHUNK = 256
ATTN_TILE = 256
STEPS_PER_TRIP = 16
POST_TILE = 512
IN_BLOCKS_PER_STEP = 4
VMEM_LIMIT = 56 * 1024 * 1024


def _extras_base(h):
    return h * LANES + (HEAD_DIM if h % 2 == 0 else 0)


def _lane_tables(slopes):
    table = np.zeros((SUBLANES, AUG), np.float32)
    table[3] = -1.0
    head_of = np.repeat(np.arange(N_HEADS), LANES)
    for h in range(N_HEADS):
        q0 = h * LANES + (0 if h % 2 == 0 else HEAD_DIM)
        table[0, q0:q0 + HEAD_DIM] = 1.0
        b = _extras_base(h) + BIAS_LANE
        table[1, b + 3:b + 6] = 1.0
        table[2, b:b + 3] = 1.0
        table[4, b:b + 3] = -1.0
        table[5, b + 3:b + 6] = 1.0
        table[6, b:b + 6] = [0, 1, 2, 0, 1, 2]
        m = _extras_base(h) + MASK_LANE
        table[3, m:m + MAX_MOBA_BLOCKS] = np.arange(MAX_MOBA_BLOCKS)
    slope_of_lane = slopes.astype(F32)[head_of] * LOG2E
    scale = jnp.ones((SUBLANES, AUG), F32).at[4].set(slope_of_lane).at[5].set(slope_of_lane)
    return jnp.asarray(table) * scale


def _placement_matrices():
    p_fox = np.zeros((LANES, 2 * AUG), np.float32)
    p_mask = np.zeros((N_HEADS * MAX_MOBA_BLOCKS, AUG), np.float32)
    for h in range(N_HEADS):
        b = _extras_base(h) + BIAS_LANE
        for k in range(3):
            p_fox[k * N_HEADS + h, b + k] = 1.0
            p_fox[k * N_HEADS + h, AUG + b + 3 + k] = -1.0
        m = _extras_base(h) + MASK_LANE
        for j in range(MAX_MOBA_BLOCKS):
            p_mask[h * MAX_MOBA_BLOCKS + j, m + j] = 1.0
    return jnp.asarray(p_fox, BF16), jnp.asarray(p_mask, BF16)


def _split3(v):
    hi = v.astype(BF16)
    r = v - hi.astype(F32)
    mid = r.astype(BF16)
    lo = (r - mid.astype(F32)).astype(BF16)
    return hi, mid, lo


def _split2(v):
    hi = v.astype(BF16)
    return hi, (v - hi.astype(F32)).astype(BF16)


def _rms(x, g):
    return x * lax.rsqrt(jnp.mean(x * x, axis=-1, keepdims=True) + EPS) * g


def _dot(a, b):
    return jnp.dot(a, b, preferred_element_type=F32)


def _dot_nt(a, b):
    return lax.dot_general(a, b, (((1,), (1,)), ((), ())), preferred_element_type=F32)


def _dup_pairs(t):
    cols = []
    for p in range(GROUP_WIDTH // LANES):
        c = t[:, p * LANES:(p + 1) * LANES]
        cols += [c, c]
    return jnp.concatenate(cols, axis=1)


def _const_spec(shape):
    return pl.BlockSpec(shape, lambda *_: (0,) * len(shape), pipeline_mode=pl.Buffered(1))


def _inproj_kernel(x_ref, g_ref, w_ref, bf_ref, tri_ref, pfox_ref, pmask_ref, meta_ref,
                   qf_ref, kf_ref, vf_ref, qm_ref, km_ref, vm_ref, carry_ref, kbdt_ref, *, blocks_per_step):
    tm = MOBA_BLOCK
    i = pl.program_id(1)

    @pl.when(i == 0)
    def _():
        carry_ref[...] = jnp.zeros_like(carry_ref)
        kbdt_ref[...] = jnp.zeros_like(kbdt_ref)

    meta = meta_ref[...]
    own = meta[0:1] > 0.5
    ones_q = meta[1:2]
    ones_k = meta[2:3]
    block_of_lane = meta[3:4]
    piece_of_lane = meta[6:7]
    head_lane = lax.broadcasted_iota(jnp.int32, (1, FLOGIT_PAD), 1) < N_HEADS
    tri = tri_ref[...]
    jrow = lax.broadcasted_iota(jnp.int32, (MAX_MOBA_BLOCKS, tm), 0)
    head_of_lane = lax.broadcasted_iota(jnp.int32, (1, GROUP_WIDTH), 1) // HEAD_DIM

    def project(rows):
        a = _rms(x_ref[0, rows, :], g_ref[...]).astype(BF16)
        cols = [_dot(a, w_ref[:, c * GROUP_WIDTH:(c + 1) * GROUP_WIDTH]) for c in range(6)]
        return cols + [_dot(a, w_ref[:, 6 * GROUP_WIDTH:6 * GROUP_WIDTH + FLOGIT_PAD])]

    def finish(rows, blk, projected):
        qf, kf, vf, qm, km, vm, z = projected

        z = z + bf_ref[...]
        log_f = jnp.minimum(z, 0.0) - jnp.log1p(jnp.exp(-jnp.abs(z)))
        f_hi, f_mid, f_lo = _split3(log_f)
        c = (_dot(tri, f_hi) + _dot(tri, f_mid) + _dot(tri, f_lo)) + carry_ref[...]
        carry_ref[...] = c[tm - 1:tm, :]
        c_hi, c_mid, c_lo = _split3(jnp.where(head_lane, c * LOG2E, 0.0))
        packed = (c_hi.astype(F32) + pltpu.roll(c_mid.astype(F32), N_HEADS, 1)
                  + pltpu.roll(c_lo.astype(F32), 2 * N_HEADS, 1)).astype(BF16)
        ext = _dot(packed, pfox_ref[...])
        qf_ref[0, rows, :] = jnp.where(own, _dup_pairs(qf * QK_SCALE), ext[:, :AUG] + ones_q).astype(BF16)
        kf_ref[0, rows, :] = jnp.where(own, _dup_pairs(kf), ext[:, AUG:] + ones_k).astype(BF16)
        vf_ref[0, rows, :] = vf.astype(BF16)

        vm_ref[0, rows, :] = vm.astype(BF16)
        kb_hi, kb_lo = _split2(kbdt_ref[...])
        q_hi, q_lo = _split2(qm)
        gate_t = _dot_nt(kb_hi, q_hi) + _dot_nt(kb_hi, q_lo) + _dot_nt(kb_lo, q_hi)
        past = jrow < blk
        masks = []
        for h in range(N_HEADS):
            g = jnp.where(past, gate_t[h * MAX_MOBA_BLOCKS:(h + 1) * MAX_MOBA_BLOCKS], -jnp.inf)
            sel = jrow == blk
            for _ in range(MOBA_TOP_K):
                top = jnp.max(g, axis=0, keepdims=True)
                cand = past & (g == top)
                first = jnp.min(jnp.where(cand, jrow, MAX_MOBA_BLOCKS), axis=0, keepdims=True)
                pick = jrow == first
                sel = sel | pick
                g = jnp.where(pick, -jnp.inf, g)
            masks.append(jnp.where(sel, 0.0, MASK_NEG))
        mask = jnp.concatenate(masks, axis=0).T.astype(BF16)
        ext_mask = _dot(mask, pmask_ref[...])

        pos = (blk * tm + lax.broadcasted_iota(jnp.int32, (tm, 1), 0)).astype(F32)

        def bias_pieces(full):
            hi = full.astype(BF16).astype(F32)
            r = full - hi
            mid = r.astype(BF16).astype(F32)
            return jnp.where(piece_of_lane == 0.0, hi, jnp.where(piece_of_lane == 1.0, mid, r - mid))

        onehot = jnp.where(block_of_lane == blk.astype(F32), 1.0, 0.0)
        qm_ref[0, rows, :] = jnp.where(own, _dup_pairs(qm * QK_SCALE),
                                       ext_mask + bias_pieces(pos * meta[4:5]) + ones_q).astype(BF16)
        km_ref[0, rows, :] = jnp.where(own, _dup_pairs(km),
                                       bias_pieces(pos * meta[5:6]) + ones_k + onehot).astype(BF16)

        kmean = jnp.mean(km, axis=0, keepdims=True)
        for h in range(N_HEADS):
            kbdt_ref[pl.ds(h * MAX_MOBA_BLOCKS + blk, 1), :] = jnp.where(head_of_lane == h, kmean, 0.0)

    row_groups = [pl.ds(g * tm, tm) for g in range(blocks_per_step)]
    ahead = project(row_groups[0])
    for g in range(blocks_per_step):
        current = ahead
        if g + 1 < blocks_per_step:
            ahead = project(row_groups[g + 1])
        finish(row_groups[g], i * blocks_per_step + g, current)


def _in_projection(x, g_mix, w_in, b_forget, slopes):
    B, S, D = x.shape
    n_blk = S // MOBA_BLOCK
    bps = IN_BLOCKS_PER_STEP if n_blk % IN_BLOCKS_PER_STEP == 0 else 1
    tm = bps * MOBA_BLOCK
    assert S % MOBA_BLOCK == 0 and MOBA_TOP_K <= n_blk <= MAX_MOBA_BLOCKS and D == D_MODEL
    c = np.cumsum([GROUP_WIDTH, GROUP_WIDTH, GROUP_WIDTH, N_HEADS, GROUP_WIDTH, GROUP_WIDTH])
    w_f = jnp.pad(w_in[:, c[2]:c[3]], ((0, 0), (0, FLOGIT_PAD - N_HEADS)))
    w_r = jnp.concatenate([w_in[:, :c[2]], w_in[:, c[3]:], w_f], axis=1).astype(BF16)
    n_cols = w_r.shape[1]
    b_f = jnp.pad(b_forget.astype(F32), (0, FLOGIT_PAD - N_HEADS)).reshape(1, FLOGIT_PAD)
    tri = jnp.asarray(np.tril(np.ones((MOBA_BLOCK, MOBA_BLOCK), np.float32)), BF16)
    p_fox, p_mask = _placement_matrices()
    meta = _lane_tables(slopes)

    aug = jax.ShapeDtypeStruct((B, S, AUG), BF16)
    val = jax.ShapeDtypeStruct((B, S, GROUP_WIDTH), BF16)
    tile = lambda w: pl.BlockSpec((1, tm, w), lambda b, i: (b, i, 0))
    return pl.pallas_call(
        functools.partial(_inproj_kernel, blocks_per_step=bps),
        grid=(B, n_blk // bps),
        in_specs=[tile(D), _const_spec((1, D)), _const_spec((D, n_cols)), _const_spec((1, FLOGIT_PAD)),
                  _const_spec(tri.shape), _const_spec(p_fox.shape),
                  _const_spec(p_mask.shape), _const_spec(meta.shape)],
        out_specs=[tile(AUG), tile(AUG), tile(GROUP_WIDTH), tile(AUG), tile(AUG), tile(GROUP_WIDTH)],
        out_shape=[aug, aug, val, aug, aug, val],
        scratch_shapes=[pltpu.VMEM((1, FLOGIT_PAD), F32),
                        pltpu.VMEM((N_HEADS * MAX_MOBA_BLOCKS, GROUP_WIDTH), F32)],
        compiler_params=pltpu.CompilerParams(dimension_semantics=("arbitrary", "arbitrary"),
                                             vmem_limit_bytes=VMEM_LIMIT),
        name="in_projection",
    )(x, g_mix.reshape(1, D), w_r, b_f, tri, p_fox, p_mask, meta)


def _attention_kernel(q_ref, k_ref, v_ref, o_ref, va_ref, vb_ref, s_ref, m_ref, acc_ref, *, seq, t):
    nq = seq // t
    reps = t // LANES
    low = lax.broadcasted_iota(jnp.int32, (1, LANES), 1) < HEAD_DIM
    vp = v_ref[0]
    one = jnp.ones((), BF16)
    va_ref[...] = jnp.where(low, vp, one)
    vb_ref[...] = jnp.where(low, one, vp)
    v_refs = (va_ref, vb_ref)
    causal = (lax.broadcasted_iota(jnp.int32, (t, t), 1) <= lax.broadcasted_iota(jnp.int32, (t, t), 0))

    def rows(tile):
        return pl.ds(pl.multiple_of(tile * t, t), t)

    def stage_a(qi, kt, slot, diag):
        for hh in range(2):
            cols = slice(hh * LANES, (hh + 1) * LANES)
            s = _dot_nt(q_ref[0, rows(qi), cols], k_ref[0, rows(kt), cols])
            s_ref[slot, hh] = jnp.where(causal, s, -jnp.inf) if diag else s

    def stage_b(qi, kt, slot, diag):
        for hh in range(2):
            s = s_ref[slot, hh]
            m_cur = jnp.max(s, axis=1, keepdims=True)
            if diag:
                m_next = jnp.broadcast_to(m_cur, (t, LANES))
            else:
                m_prev = m_ref[hh, rows(qi)]
                m_next = jnp.maximum(m_prev, m_cur)
            p = jnp.exp2(s - jnp.concatenate([m_next] * reps, axis=1)).astype(BF16)
            pv = _dot(p, v_refs[hh][rows(kt), :])
            acc_ref[hh, rows(qi)] = pv if diag else jnp.exp2(m_prev - m_next) * acc_ref[hh, rows(qi)] + pv
            m_ref[hh, rows(qi)] = m_next

    def run_pipeline(n_steps, first, advance, diag, unroll):
        def step(parity, cs):
            ca, cb = cs
            stage_a(*ca, parity, diag)
            stage_b(*cb, 1 - parity, diag)
            return advance(*ca), advance(*cb)

        def trip(_, cs):
            for u in range(unroll):
                cs = step(u % 2, cs)
            return cs

        stage_a(*first, 1, diag)
        cs = (advance(*first), first)
        n_steady = n_steps - 1
        cs = lax.fori_loop(0, n_steady // unroll, trip, cs)
        f = unroll * (n_steady // unroll)
        for _ in range(n_steady % unroll):
            cs = step(f % 2, cs)
            f += 1
        stage_b(*cs[1], 1 - f % 2, diag)

    def next_below_diagonal(qi, kt):
        wrap = kt + 1 >= qi
        return jnp.where(wrap, qi + 1, qi), jnp.where(wrap, 0, kt + 1)

    i32 = jnp.int32
    run_pipeline(nq, (i32(0), i32(0)), lambda qi, kt: (qi + 1, kt + 1), diag=True, unroll=2)
    run_pipeline(nq * (nq - 1) // 2, (i32(1), i32(0)), next_below_diagonal, diag=False, unroll=STEPS_PER_TRIP)

    def normalise(tile, carry):
        a0 = acc_ref[0, rows(tile)]
        a1 = acc_ref[1, rows(tile)]
        den = pltpu.roll(jnp.where(low, a1, a0), HEAD_DIM, 1)
        o_ref[0, rows(tile), :] = jnp.where(low, a0, a1) / den
        return carry

    lax.fori_loop(0, nq, normalise, 0)


def _attention(q_aug, k_aug, v):
    B, S, _ = q_aug.shape
    t = ATTN_TILE if S % ATTN_TILE == 0 and S // ATTN_TILE >= 3 else MOBA_BLOCK
    assert S % t == 0 and S // t >= 3
    pair = lambda w: pl.BlockSpec((1, S, w), lambda b, p: (b, 0, p))
    return pl.pallas_call(
        functools.partial(_attention_kernel, seq=S, t=t),
        grid=(B, N_HEADS // 2),
        in_specs=[pair(2 * LANES), pair(2 * LANES), pair(LANES)],
        out_specs=pair(LANES),
        out_shape=jax.ShapeDtypeStruct((B, S, GROUP_WIDTH), F32),
        scratch_shapes=[pltpu.VMEM((S, LANES), BF16), pltpu.VMEM((S, LANES), BF16),
                        pltpu.VMEM((2, 2, t, t), F32),
                        pltpu.VMEM((2, S, LANES), F32),
                        pltpu.VMEM((2, S, LANES), F32)],
        compiler_params=pltpu.CompilerParams(dimension_semantics=("arbitrary", "arbitrary"),
                                             vmem_limit_bytes=VMEM_LIMIT),
        name="pair_attention",
    )(q_aug, k_aug, v)


def _mem_kv_kernel(mem_ref, g_ref, w_ref, kv_ref):
    kv_ref[0] = _dot(_rms(mem_ref[0], g_ref[...]).astype(BF16), w_ref[...]).astype(BF16)


def _memory_kv(mem, g_mem, w_kv):
    B, M, D = mem.shape
    return pl.pallas_call(
        _mem_kv_kernel,
        grid=(B,),
        in_specs=[pl.BlockSpec((1, M, D), lambda b: (b, 0, 0)), _const_spec((1, D)), _const_spec((D, 2 * D))],
        out_specs=pl.BlockSpec((1, M, 2 * D), lambda b: (b, 0, 0)),
        out_shape=jax.ShapeDtypeStruct((B, M, 2 * D), BF16),
        compiler_params=pltpu.CompilerParams(dimension_semantics=("arbitrary",), vmem_limit_bytes=VMEM_LIMIT),
        name="memory_kv",
    )(mem, g_mem.reshape(1, D), w_kv.astype(BF16))


def _post_kernel(of_ref, om_ref, x_ref, kv_ref, gfox_ref, gmoba_ref, gcross_ref, gffn_ref, gfin_ref,
                 wout_ref, wq_ref, wo_ref, wgu_ref, cw_ref, wd_ref, out_ref,
                 halo_ref, *, tm, n_chunks, final_norm):
    i = pl.program_id(1)

    @pl.when(i == 0)
    def _():
        halo_ref[...] = jnp.zeros_like(halo_ref)

    yf = _rms(of_ref[0], gfox_ref[...]).astype(BF16)
    ym = _rms(om_ref[0], gmoba_ref[...]).astype(BF16)
    h = x_ref[0] + _dot(yf, wout_ref[0:GROUP_WIDTH, :]) + _dot(ym, wout_ref[GROUP_WIDTH:2 * GROUP_WIDTH, :])

    qc = (_dot(_rms(h, gcross_ref[...]).astype(BF16), wq_ref[...]) * (CROSS_HEAD_DIM ** -0.5)).astype(BF16)
    heads = []
    for hd in range(N_CROSS_HEADS):
        lo = hd * CROSS_HEAD_DIM
        s = _dot_nt(qc[:, lo:lo + CROSS_HEAD_DIM], kv_ref[0, :, lo:lo + CROSS_HEAD_DIM])
        p = jnp.exp(s - jnp.max(s, axis=1, keepdims=True))
        o = _dot(p.astype(BF16), kv_ref[0, :, D_MODEL + lo:D_MODEL + lo + CROSS_HEAD_DIM])
        heads.append((o / jnp.sum(p, axis=1, keepdims=True)).astype(BF16))
    h = h + _dot(jnp.concatenate(heads, axis=1), wo_ref[...])

    a3 = _rms(h, gffn_ref[...]).astype(BF16)
    first_rows = lax.broadcasted_iota(jnp.int32, (SUBLANES, FFN_CHUNK), 0)

    def delayed(g, prev_tail, d):
        rolled = pltpu.roll(g, d, 0)
        head = jnp.where(first_rows < d, pltpu.roll(prev_tail, d, 0), rolled[:SUBLANES])
        return jnp.concatenate([head, rolled[SUBLANES:]], axis=0)

    gu_next = _dot(a3, wgu_ref[0])
    for c in range(n_chunks):
        gu = gu_next
        if c + 1 < n_chunks:
            gu_next = _dot(a3, wgu_ref[c + 1])
        g = gu[:, :FFN_CHUNK]
        prev_tail = halo_ref[c]
        halo_ref[c] = g[tm - SUBLANES:tm, :]
        cw = cw_ref[c]
        gc = cw[3:4] + cw[0:1] * delayed(g, prev_tail, 2) + cw[1:2] * delayed(g, prev_tail, 1) + cw[2:3] * g
        half = 0.5 * gc
        act = (half * (1.0 + jnp.tanh(half)) * gu[:, FFN_CHUNK:]).astype(BF16)
        h = h + _dot(act, wd_ref[c])
    out_ref[0] = _rms(h, gfin_ref[...]) if final_norm else h


def _post_attention(o_f, o_m, x, kv, g_fox, g_moba, w_out, g_cross, w_q, w_o, g_ffn, w_up, conv_w, conv_b,
                    w_down, g_final, final_norm):
    B, S, D = x.shape
    M = kv.shape[1]
    tm = POST_TILE
    d_ff = w_down.shape[0]
    assert d_ff % FFN_CHUNK == 0 and S % tm == 0
    nc = d_ff // FFN_CHUNK
    w_g = w_up[:, :d_ff].reshape(D, nc, FFN_CHUNK)
    w_u = w_up[:, d_ff:].reshape(D, nc, FFN_CHUNK)
    w_gu = jnp.concatenate([w_g, w_u], axis=2).transpose(1, 0, 2).astype(BF16)
    w_d = w_down.reshape(nc, FFN_CHUNK, D).astype(BF16)
    cw = jnp.concatenate([conv_w, conv_b[None, :], jnp.zeros((SUBLANES - CONV_WIDTH - 1, d_ff), F32)], axis=0)
    cw = cw.reshape(SUBLANES, nc, FFN_CHUNK).transpose(1, 0, 2)
    row = lambda g: g.reshape(1, -1).astype(F32)
    tile = lambda w: pl.BlockSpec((1, tm, w), lambda b, i: (b, i, 0))
    return pl.pallas_call(
        functools.partial(_post_kernel, tm=tm, n_chunks=nc, final_norm=final_norm),
        grid=(B, S // tm),
        in_specs=[tile(GROUP_WIDTH), tile(GROUP_WIDTH), tile(D),
                  pl.BlockSpec((1, M, 2 * D), lambda b, i: (b, 0, 0)),
                  _const_spec((1, GROUP_WIDTH)), _const_spec((1, GROUP_WIDTH)), _const_spec((1, D)),
                  _const_spec((1, D)), _const_spec((1, D)),
                  _const_spec((2 * GROUP_WIDTH, D)), _const_spec((D, D)), _const_spec((D, D)),
                  _const_spec(w_gu.shape), _const_spec(cw.shape), _const_spec(w_d.shape)],
        out_specs=tile(D),
        out_shape=jax.ShapeDtypeStruct((B, S, D), F32),
        scratch_shapes=[pltpu.VMEM((nc, SUBLANES, FFN_CHUNK), F32)],
        compiler_params=pltpu.CompilerParams(dimension_semantics=("arbitrary", "arbitrary"),
                                             vmem_limit_bytes=VMEM_LIMIT),
        name="post_attention",
    )(o_f, o_m, x, kv, row(g_fox), row(g_moba), row(g_cross), row(g_ffn), row(g_final),
      w_out.astype(BF16), w_q.astype(BF16), w_o.astype(BF16), w_gu, cw, w_d)


def kernel(x, mem, g_mix, w_in, b_forget, g_fox, g_moba, w_out, g_cross, g_mem, w_q_mem, w_kv_mem,
           w_o_mem, g_ffn, w_up, conv_w, conv_b, w_down, g_final):
    depth = g_mix.shape[0]
    slopes = 2.0 ** (-8.0 * jnp.arange(1, N_HEADS + 1, dtype=F32) / N_HEADS)
    h = x
    for l in range(depth):
        qf, kf, vf, qm, km, vm = _in_projection(h, g_mix[l], w_in[l], b_forget[l], slopes)
        o_f = _attention(qf, kf, vf)
        o_m = _attention(qm, km, vm)
        kv = _memory_kv(mem, g_mem[l], w_kv_mem[l])
        h = _post_attention(o_f, o_m, h, kv, g_fox[l], g_moba[l], w_out[l], g_cross[l], w_q_mem[l],
                            w_o_mem[l], g_ffn[l], w_up[l], conv_w[l], conv_b[l], w_down[l], g_final,
                            final_norm=(l == depth - 1))
    return h
```

```python
import functools

import numpy as np
import jax
import jax.numpy as jnp
from jax import lax
from jax.experimental import pallas as pl
from jax.experimental.pallas import tpu as pltpu

F32 = jnp.float32
BF16 = jnp.bfloat16

D_MODEL = 1024
HEAD_DIM = 64
N_HEADS = 8
GROUP_WIDTH = N_HEADS * HEAD_DIM
MOBA_BLOCK = 256
MOBA_TOP_K = 3
MAX_MOBA_BLOCKS = 32
N_CROSS_HEADS = 4
CROSS_HEAD_DIM = D_MODEL // N_CROSS_HEADS
CONV_WIDTH = 3
EPS = 1e-6

LANES = 128
SUBLANES = 8
AUG = N_HEADS * LANES
FLOGIT_PAD = LANES
MASK_NEG = -2.0 ** 100
LOG2E = 1.4426950408889634
QK_SCALE = HEAD_DIM ** -0.5 * LOG2E
BIAS_LANE = 0
MASK_LANE = 8
FFN_CHUNK = 256
ATTN_TILE = 256
STEPS_PER_TRIP = 16
POST_TILE = 512
IN_BLOCKS_PER_STEP = 4
VMEM_LIMIT = 56 * 1024 * 1024


def _extras_base(h):
    return h * LANES + (HEAD_DIM if h % 2 == 0 else 0)


def _lane_tables(slopes):
    table = np.zeros((SUBLANES, AUG), np.float32)
    table[3] = -1.0
    head_of = np.repeat(np.arange(N_HEADS), LANES)
    for h in range(N_HEADS):
        q0 = h * LANES + (0 if h % 2 == 0 else HEAD_DIM)
        table[0, q0:q0 + HEAD_DIM] = 1.0
        b = _extras_base(h) + BIAS_LANE
        table[1, b + 3:b + 6] = 1.0
        table[2, b:b + 3] = 1.0
        table[4, b:b + 3] = -1.0
        table[5, b + 3:b + 6] = 1.0
        table[6, b:b + 6] = [0, 1, 2, 0, 1, 2]
        m = _extras_base(h) + MASK_LANE
        table[3, m:m + MAX_MOBA_BLOCKS] = np.arange(MAX_MOBA_BLOCKS)
    slope_of_lane = slopes.astype(F32)[head_of] * LOG2E
    scale = jnp.ones((SUBLANES, AUG), F32).at[4].set(slope_of_lane).at[5].set(slope_of_lane)
    return jnp.asarray(table) * scale


def _placement_matrices():
    p_fox = np.zeros((LANES, 2 * AUG), np.float32)
    p_mask = np.zeros((N_HEADS * MAX_MOBA_BLOCKS, AUG), np.float32)
    for h in range(N_HEADS):
        b = _extras_base(h) + BIAS_LANE
        for k in range(3):
            p_fox[k * N_HEADS + h, b + k] = 1.0
            p_fox[k * N_HEADS + h, AUG + b + 3 + k] = -1.0
        m = _extras_base(h) + MASK_LANE
        for j in range(MAX_MOBA_BLOCKS):
            p_mask[h * MAX_MOBA_BLOCKS + j, m + j] = 1.0
    return jnp.asarray(p_fox, BF16), jnp.asarray(p_mask, BF16)


def _split3(v):
    hi = v.astype(BF16)
    r = v - hi.astype(F32)
    mid = r.astype(BF16)
    lo = (r - mid.astype(F32)).astype(BF16)
    return hi, mid, lo


def _split2(v):
    hi = v.astype(BF16)
    return hi, (v - hi.astype(F32)).astype(BF16)


def _rms(x, g):
    return x * lax.rsqrt(jnp.mean(x * x, axis=-1, keepdims=True) + EPS) * g


def _dot(a, b):
    return jnp.dot(a, b, preferred_element_type=F32)


def _dot_nt(a, b):
    return lax.dot_general(a, b, (((1,), (1,)), ((), ())), preferred_element_type=F32)


def _dup_pairs(t):
    cols = []
    for p in range(GROUP_WIDTH // LANES):
        c = t[:, p * LANES:(p + 1) * LANES]
        cols += [c, c]
    return jnp.concatenate(cols, axis=1)


def _const_spec(shape):
    return pl.BlockSpec(shape, lambda *_: (0,) * len(shape), pipeline_mode=pl.Buffered(1))


def _inproj_kernel(x_ref, g_ref, w_ref, bf_ref, tri_ref, pfox_ref, pmask_ref, meta_ref,
                   qf_ref, kf_ref, vf_ref, qm_ref, km_ref, vm_ref, carry_ref, kbdt_ref, *, blocks_per_step):
    tm = MOBA_BLOCK
    i = pl.program_id(1)

    @pl.when(i == 0)
    def _():
        carry_ref[...] = jnp.zeros_like(carry_ref)
        kbdt_ref[...] = jnp.zeros_like(kbdt_ref)

    meta = meta_ref[...]
    own = meta[0:1] > 0.5
    ones_q = meta[1:2]
    ones_k = meta[2:3]
    block_of_lane = meta[3:4]
    piece_of_lane = meta[6:7]
    head_lane = lax.broadcasted_iota(jnp.int32, (1, FLOGIT_PAD), 1) < N_HEADS
    tri = tri_ref[...]
    jrow = lax.broadcasted_iota(jnp.int32, (MAX_MOBA_BLOCKS, tm), 0)
    head_of_lane = lax.broadcasted_iota(jnp.int32, (1, GROUP_WIDTH), 1) // HEAD_DIM

    def project(rows):
        a = _rms(x_ref[0, rows, :], g_ref[...]).astype(BF16)
        cols = [_dot(a, w_ref[:, c * GROUP_WIDTH:(c + 1) * GROUP_WIDTH]) for c in range(6)]
        return cols + [_dot(a, w_ref[:, 6 * GROUP_WIDTH:6 * GROUP_WIDTH + FLOGIT_PAD])]

    def finish(rows, blk, projected):
        qf, kf, vf, qm, km, vm, z = projected

        z = z + bf_ref[...]
        log_f = jnp.minimum(z, 0.0) - jnp.log1p(jnp.exp(-jnp.abs(z)))
        f_hi, f_mid, f_lo = _split3(log_f)
        c = (_dot(tri, f_hi) + _dot(tri, f_mid) + _dot(tri, f_lo)) + carry_ref[...]
        carry_ref[...] = c[tm - 1:tm, :]
        c_hi, c_mid, c_lo = _split3(jnp.where(head_lane, c * LOG2E, 0.0))
        packed = (c_hi.astype(F32) + pltpu.roll(c_mid.astype(F32), N_HEADS, 1)
                  + pltpu.roll(c_lo.astype(F32), 2 * N_HEADS, 1)).astype(BF16)
        ext = _dot(packed, pfox_ref[...])
        qf_ref[0, rows, :] = jnp.where(own, _dup_pairs(qf * QK_SCALE), ext[:, :AUG] + ones_q).astype(BF16)
        kf_ref[0, rows, :] = jnp.where(own, _dup_pairs(kf), ext[:, AUG:] + ones_k).astype(BF16)
        vf_ref[0, rows, :] = vf.astype(BF16)

        vm_ref[0, rows, :] = vm.astype(BF16)
        kb_hi, kb_lo = _split2(kbdt_ref[...])
        q_hi, q_lo = _split2(qm)
        gate_t = _dot_nt(kb_hi, q_hi) + _dot_nt(kb_hi, q_lo) + _dot_nt(kb_lo, q_hi)
        past = jrow < blk
        masks = []
        for h in range(N_HEADS):
            g = jnp.where(past, gate_t[h * MAX_MOBA_BLOCKS:(h + 1) * MAX_MOBA_BLOCKS], -jnp.inf)
            sel = jrow == blk
            for _ in range(MOBA_TOP_K):
                top = jnp.max(g, axis=0, keepdims=True)
                cand = past & (g == top)
                first = jnp.min(jnp.where(cand, jrow, MAX_MOBA_BLOCKS), axis=0, keepdims=True)
                pick = jrow == first
                sel = sel | pick
                g = jnp.where(pick, -jnp.inf, g)
            masks.append(jnp.where(sel, 0.0, MASK_NEG))
        mask = jnp.concatenate(masks, axis=0).T.astype(BF16)
        ext_mask = _dot(mask, pmask_ref[...])

        pos = (blk * tm + lax.broadcasted_iota(jnp.int32, (tm, 1), 0)).astype(F32)

        def bias_pieces(full):
            hi = full.astype(BF16).astype(F32)
            r = full - hi
            mid = r.astype(BF16).astype(F32)
            return jnp.where(piece_of_lane == 0.0, hi, jnp.where(piece_of_lane == 1.0, mid, r - mid))

        onehot = jnp.where(block_of_lane == blk.astype(F32), 1.0, 0.0)
        qm_ref[0, rows, :] = jnp.where(own, _dup_pairs(qm * QK_SCALE),
                                       ext_mask + bias_pieces(pos * meta[4:5]) + ones_q).astype(BF16)
        km_ref[0, rows, :] = jnp.where(own, _dup_pairs(km),
                                       bias_pieces(pos * meta[5:6]) + ones_k + onehot).astype(BF16)

        kmean = jnp.mean(km, axis=0, keepdims=True)
        for h in range(N_HEADS):
            kbdt_ref[pl.ds(h * MAX_MOBA_BLOCKS + blk, 1), :] = jnp.where(head_of_lane == h, kmean, 0.0)

    row_groups = [pl.ds(g * tm, tm) for g in range(blocks_per_step)]
    ahead = project(row_groups[0])
    for g in range(blocks_per_step):
        current = ahead
        if g + 1 < blocks_per_step:
            ahead = project(row_groups[g + 1])
        finish(row_groups[g], i * blocks_per_step + g, current)


def _in_projection(x, g_mix, w_in, b_forget, slopes):
    B, S, D = x.shape
    n_blk = S // MOBA_BLOCK
    bps = IN_BLOCKS_PER_STEP if n_blk % IN_BLOCKS_PER_STEP == 0 else 1
    tm = bps * MOBA_BLOCK
    assert S % MOBA_BLOCK == 0 and MOBA_TOP_K <= n_blk <= MAX_MOBA_BLOCKS and D == D_MODEL
    c = np.cumsum([GROUP_WIDTH, GROUP_WIDTH, GROUP_WIDTH, N_HEADS, GROUP_WIDTH, GROUP_WIDTH])
    w_f = jnp.pad(w_in[:, c[2]:c[3]], ((0, 0), (0, FLOGIT_PAD - N_HEADS)))
    w_r = jnp.concatenate([w_in[:, :c[2]], w_in[:, c[3]:], w_f], axis=1).astype(BF16)
    n_cols = w_r.shape[1]
    b_f = jnp.pad(b_forget.astype(F32), (0, FLOGIT_PAD - N_HEADS)).reshape(1, FLOGIT_PAD)
    tri = jnp.asarray(np.tril(np.ones((MOBA_BLOCK, MOBA_BLOCK), np.float32)), BF16)
    p_fox, p_mask = _placement_matrices()
    meta = _lane_tables(slopes)

    aug = jax.ShapeDtypeStruct((B, S, AUG), BF16)
    val = jax.ShapeDtypeStruct((B, S, GROUP_WIDTH), BF16)
    tile = lambda w: pl.BlockSpec((1, tm, w), lambda b, i: (b, i, 0))
    return pl.pallas_call(
        functools.partial(_inproj_kernel, blocks_per_step=bps),
        grid=(B, n_blk // bps),
        in_specs=[tile(D), _const_spec((1, D)), _const_spec((D, n_cols)), _const_spec((1, FLOGIT_PAD)),
                  _const_spec(tri.shape), _const_spec(p_fox.shape),
                  _const_spec(p_mask.shape), _const_spec(meta.shape)],
        out_specs=[tile(AUG), tile(AUG), tile(GROUP_WIDTH), tile(AUG), tile(AUG), tile(GROUP_WIDTH)],
        out_shape=[aug, aug, val, aug, aug, val],
        scratch_shapes=[pltpu.VMEM((1, FLOGIT_PAD), F32),
                        pltpu.VMEM((N_HEADS * MAX_MOBA_BLOCKS, GROUP_WIDTH), F32)],
        compiler_params=pltpu.CompilerParams(dimension_semantics=("arbitrary", "arbitrary"),
                                             vmem_limit_bytes=VMEM_LIMIT),
        name="in_projection",
    )(x, g_mix.reshape(1, D), w_r, b_f, tri, p_fox, p_mask, meta)


def _attention_kernel(q_ref, k_ref, v_ref, o_ref, vt_ref, s_ref, p_ref, alpha_ref, m_ref, acc_ref, *, seq, t):
    nq = seq // t
    low = lax.broadcasted_iota(jnp.int32, (1, LANES), 1) < HEAD_DIM
    one = jnp.ones((), F32)

    def rows(tile):
        return pl.ds(pl.multiple_of(tile * t, t), t)

    def transpose_values(tile, carry):
        vp = v_ref[0, rows(tile), :].astype(F32)
        vt_ref[0, tile] = jnp.where(low, vp, one).T.astype(BF16)
        vt_ref[1, tile] = jnp.where(low, one, vp).T.astype(BF16)
        return carry

    lax.fori_loop(0, nq, transpose_values, 0)
    causal = (lax.broadcasted_iota(jnp.int32, (t, t), 0) <= lax.broadcasted_iota(jnp.int32, (t, t), 1))

    def stage_a(qi, kt, slot, diag):
        for hh in range(2):
            cols = slice(hh * LANES, (hh + 1) * LANES)
            s = _dot_nt(k_ref[0, rows(kt), cols], q_ref[0, rows(qi), cols])
            s_ref[slot, hh] = jnp.where(causal, s, -jnp.inf) if diag else s

    def stage_b(qi, kt, slot, diag):
        for hh in range(2):
            s = s_ref[slot, hh]
            m_cur = jnp.max(s, axis=0, keepdims=True)
            if diag:
                m_next = m_cur
            else:
                m_prev = m_ref[hh, qi]
                m_next = jnp.maximum(m_prev, m_cur)
                alpha_ref[slot, hh] = jnp.exp2(m_prev - m_next)
            m_ref[hh, qi] = m_next
            p_ref[slot, hh] = jnp.exp2(s - m_next).astype(BF16)

    def stage_c(qi, kt, slot, diag):
        for hh in range(2):
            pv = _dot(vt_ref[hh, kt], p_ref[slot, hh])
            acc_ref[hh, qi] = pv if diag else alpha_ref[slot, hh] * acc_ref[hh, qi] + pv

    def run_pipeline(n_steps, first, advance, diag, unroll):
        def step(parity, cs):
            ca, cb, cc = cs
            stage_a(*ca, parity, diag)
            stage_b(*cb, 1 - parity, diag)
            stage_c(*cc, parity, diag)
            return advance(*ca), advance(*cb), advance(*cc)

        def trip(_, cs):
            for u in range(unroll):
                cs = step(u % 2, cs)
            return cs

        c0 = first
        c1 = advance(*c0)
        stage_a(*c0, 0, diag)
        stage_a(*c1, 1, diag)
        stage_b(*c0, 0, diag)
        cs = (advance(*c1), c1, c0)
        n_steady = n_steps - 2
        cs = lax.fori_loop(0, n_steady // unroll, trip, cs)
        f = 2 + unroll * (n_steady // unroll)
        for _ in range(n_steady % unroll):
            cs = step(f % 2, cs)
            f += 1
        _, cb, cc = cs
        stage_b(*cb, (f - 1) % 2, diag)
        stage_c(*cc, f % 2, diag)
        stage_c(*advance(*cc), (f + 1) % 2, diag)

    def next_below_diagonal(qi, kt):
        wrap = kt + 1 >= qi
        return jnp.where(wrap, qi + 1, qi), jnp.where(wrap, 0, kt + 1)

    i32 = jnp.int32
    run_pipeline(nq, (i32(0), i32(0)), lambda qi, kt: (qi + 1, kt + 1), diag=True, unroll=2)
    run_pipeline(nq * (nq - 1) // 2, (i32(1), i32(0)), next_below_diagonal, diag=False, unroll=STEPS_PER_TRIP)

    def normalise(tile, carry):
        a0 = acc_ref[0, tile]
        a1 = acc_ref[1, tile]
        out_t = jnp.concatenate([a0[:HEAD_DIM] / a0[HEAD_DIM:HEAD_DIM + 1], a1[HEAD_DIM:] / a1[0:1]], axis=0)
        o_ref[0, rows(tile), :] = out_t.T
        return carry

    lax.fori_loop(0, nq, normalise, 0)


def _attention(q_aug, k_aug, v):
    B, S, _ = q_aug.shape
    t = ATTN_TILE if S % ATTN_TILE == 0 and S // ATTN_TILE >= 3 else MOBA_BLOCK
    assert S % t == 0 and S // t >= 3
    pair = lambda w: pl.BlockSpec((1, S, w), lambda b, p: (b, 0, p))
    return pl.pallas_call(
        functools.partial(_attention_kernel, seq=S, t=t),
        grid=(B, N_HEADS // 2),
        in_specs=[pair(2 * LANES), pair(2 * LANES), pair(LANES)],
        out_specs=pair(LANES),
        out_shape=jax.ShapeDtypeStruct((B, S, GROUP_WIDTH), F32),
        scratch_shapes=[pltpu.VMEM((2, S // t, LANES, t), BF16),
                        pltpu.VMEM((2, 2, t, t), F32),
                        pltpu.VMEM((2, 2, t, t), BF16),
                        pltpu.VMEM((2, 2, 1, t), F32),
                        pltpu.VMEM((2, S // t, 1, t), F32),
                        pltpu.VMEM((2, S // t, LANES, t), F32)],
        compiler_params=pltpu.CompilerParams(dimension_semantics=("arbitrary", "arbitrary"),
                                             vmem_limit_bytes=VMEM_LIMIT),
        name="pair_attention",
    )(q_aug, k_aug, v)


def _mem_kv_kernel(mem_ref, g_ref, w_ref, kv_ref):
    kv_ref[0] = _dot(_rms(mem_ref[0], g_ref[...]).astype(BF16), w_ref[...]).astype(BF16)


def _memory_kv(mem, g_mem, w_kv):
    B, M, D = mem.shape
    return pl.pallas_call(
        _mem_kv_kernel,
        grid=(B,),
        in_specs=[pl.BlockSpec((1, M, D), lambda b: (b, 0, 0)), _const_spec((1, D)), _const_spec((D, 2 * D))],
        out_specs=pl.BlockSpec((1, M, 2 * D), lambda b: (b, 0, 0)),
        out_shape=jax.ShapeDtypeStruct((B, M, 2 * D), BF16),
        compiler_params=pltpu.CompilerParams(dimension_semantics=("arbitrary",), vmem_limit_bytes=VMEM_LIMIT),
        name="memory_kv",
    )(mem, g_mem.reshape(1, D), w_kv.astype(BF16))


def _post_kernel(of_ref, om_ref, x_ref, kv_ref, gfox_ref, gmoba_ref, gcross_ref, gffn_ref, gfin_ref,
                 wout_ref, wq_ref, wo_ref, wgu_ref, cw_ref, wd_ref, out_ref,
                 halo_ref, *, tm, n_chunks, final_norm):
    i = pl.program_id(1)

    @pl.when(i == 0)
    def _():
        halo_ref[...] = jnp.zeros_like(halo_ref)

    yf = _rms(of_ref[0], gfox_ref[...]).astype(BF16)
    ym = _rms(om_ref[0], gmoba_ref[...]).astype(BF16)
    h = x_ref[0] + _dot(yf, wout_ref[0:GROUP_WIDTH, :]) + _dot(ym, wout_ref[GROUP_WIDTH:2 * GROUP_WIDTH, :])

    qc = (_dot(_rms(h, gcross_ref[...]).astype(BF16), wq_ref[...]) * (CROSS_HEAD_DIM ** -0.5)).astype(BF16)
    heads = []
    for hd in range(N_CROSS_HEADS):
        lo = hd * CROSS_HEAD_DIM
        s = _dot_nt(qc[:, lo:lo + CROSS_HEAD_DIM], kv_ref[0, :, lo:lo + CROSS_HEAD_DIM])
        p = jnp.exp(s - jnp.max(s, axis=1, keepdims=True))
        o = _dot(p.astype(BF16), kv_ref[0, :, D_MODEL + lo:D_MODEL + lo + CROSS_HEAD_DIM])
        heads.append((o / jnp.sum(p, axis=1, keepdims=True)).astype(BF16))
    h = h + _dot(jnp.concatenate(heads, axis=1), wo_ref[...])

    a3 = _rms(h, gffn_ref[...]).astype(BF16)
    first_rows = lax.broadcasted_iota(jnp.int32, (SUBLANES, FFN_CHUNK), 0)

    def delayed(g, prev_tail, d):
        rolled = pltpu.roll(g, d, 0)
        head = jnp.where(first_rows < d, pltpu.roll(prev_tail, d, 0), rolled[:SUBLANES])
        return jnp.concatenate([head, rolled[SUBLANES:]], axis=0)

    gu_next = _dot(a3, wgu_ref[0])
    for c in range(n_chunks):
        gu = gu_next
        if c + 1 < n_chunks:
            gu_next = _dot(a3, wgu_ref[c + 1])
        g = gu[:, :FFN_CHUNK]
        prev_tail = halo_ref[c]
        halo_ref[c] = g[tm - SUBLANES:tm, :]
        cw = cw_ref[c]
        gc = cw[3:4] + cw[0:1] * delayed(g, prev_tail, 2) + cw[1:2] * delayed(g, prev_tail, 1) + cw[2:3] * g
        half = 0.5 * gc
        act = (half * (1.0 + jnp.tanh(half)) * gu[:, FFN_CHUNK:]).astype(BF16)
        h = h + _dot(act, wd_ref[c])
    out_ref[0] = _rms(h, gfin_ref[...]) if final_norm else h


def _post_attention(o_f, o_m, x, kv, g_fox, g_moba, w_out, g_cross, w_q, w_o, g_ffn, w_up, conv_w, conv_b,
                    w_down, g_final, final_norm):
    B, S, D = x.shape
    M = kv.shape[1]
    tm = POST_TILE
    d_ff = w_down.shape[0]
    assert d_ff % FFN_CHUNK == 0 and S % tm == 0
    nc = d_ff // FFN_CHUNK
    w_g = w_up[:, :d_ff].reshape(D, nc, FFN_CHUNK)
    w_u = w_up[:, d_ff:].reshape(D, nc, FFN_CHUNK)
    w_gu = jnp.concatenate([w_g, w_u], axis=2).transpose(1, 0, 2).astype(BF16)
    w_d = w_down.reshape(nc, FFN_CHUNK, D).astype(BF16)
    cw = jnp.concatenate([conv_w, conv_b[None, :], jnp.zeros((SUBLANES - CONV_WIDTH - 1, d_ff), F32)], axis=0)
    cw = cw.reshape(SUBLANES, nc, FFN_CHUNK).transpose(1, 0, 2)
    row = lambda g: g.reshape(1, -1).astype(F32)
    tile = lambda w: pl.BlockSpec((1, tm, w), lambda b, i: (b, i, 0))
    return pl.pallas_call(
        functools.partial(_post_kernel, tm=tm, n_chunks=nc, final_norm=final_norm),
        grid=(B, S // tm),
        in_specs=[tile(GROUP_WIDTH), tile(GROUP_WIDTH), tile(D),
                  pl.BlockSpec((1, M, 2 * D), lambda b, i: (b, 0, 0)),
                  _const_spec((1, GROUP_WIDTH)), _const_spec((1, GROUP_WIDTH)), _const_spec((1, D)),
                  _const_spec((1, D)), _const_spec((1, D)),
                  _const_spec((2 * GROUP_WIDTH, D)), _const_spec((D, D)), _const_spec((D, D)),
                  _const_spec(w_gu.shape), _const_spec(cw.shape), _const_spec(w_d.shape)],
        out_specs=tile(D),
        out_shape=jax.ShapeDtypeStruct((B, S, D), F32),
        scratch_shapes=[pltpu.VMEM((nc, SUBLANES, FFN_CHUNK), F32)],
        compiler_params=pltpu.CompilerParams(dimension_semantics=("arbitrary", "arbitrary"),
                                             vmem_limit_bytes=VMEM_LIMIT),
        name="post_attention",
    )(o_f, o_m, x, kv, row(g_fox), row(g_moba), row(g_cross), row(g_ffn), row(g_final),
      w_out.astype(BF16), w_q.astype(BF16), w_o.astype(BF16), w_gu, cw, w_d)


def kernel(x, mem, g_mix, w_in, b_forget, g_fox, g_moba, w_out, g_cross, g_mem, w_q_mem, w_kv_mem,
           w_o_mem, g_ffn, w_up, conv_w, conv_b, w_down, g_final):
    depth = g_mix.shape[0]
    slopes = 2.0 ** (-8.0 * jnp.arange(1, N_HEADS + 1, dtype=F32) / N_HEADS)
    h = x
    for l in range(depth):
        qf, kf, vf, qm, km, vm = _in_projection(h, g_mix[l], w_in[l], b_forget[l], slopes)
        o_f = _attention(qf, kf, vf)
        o_m = _attention(qm, km, vm)
        kv = _memory_kv(mem, g_mem[l], w_kv_mem[l])
        h = _post_attention(o_f, o_m, h, kv, g_fox[l], g_moba[l], w_out[l], g_cross[l], w_q_mem[l],
                            w_o_mem[l], g_ffn[l], w_up[l], conv_w[l], conv_b[l], w_down[l], g_final,
                            final_norm=(l == depth - 1))
    return h
```

```python
import functools

import numpy as np
import jax
import jax.numpy as jnp
from jax import lax
from jax.experimental import pallas as pl
from jax.experimental.pallas import tpu as pltpu

F32 = jnp.float32
BF16 = jnp.bfloat16

D_MODEL = 1024
HEAD_DIM = 64
N_HEADS = 8
GROUP_WIDTH = N_HEADS * HEAD_DIM
MOBA_BLOCK = 256
MOBA_TOP_K = 3
MAX_MOBA_BLOCKS = 32
N_CROSS_HEADS = 4
CROSS_HEAD_DIM = D_MODEL // N_CROSS_HEADS
CONV_WIDTH = 3
EPS = 1e-6

LANES = 128
SUBLANES = 8
AUG = N_HEADS * LANES
FLOGIT_PAD = LANES
MASK_NEG = -2.0 ** 100
LOG2E = 1.4426950408889634
QK_SCALE = HEAD_DIM ** -0.5 * LOG2E
BIAS_LANE = 0
MASK_LANE = 8
FFN_CHUNK = 256
ATTN_TILE = 256
STEPS_PER_TRIP = 32
SWEEP_UNROLL = 4
POST_TILE = 512
IN_BLOCKS_PER_STEP = 4
VMEM_LIMIT = 56 * 1024 * 1024


def _extras_base(h):
    return h * LANES + (HEAD_DIM if h % 2 == 0 else 0)


def _lane_tables(slopes):
    table = np.zeros((SUBLANES, AUG), np.float32)
    table[3] = -1.0
    head_of = np.repeat(np.arange(N_HEADS), LANES)
    for h in range(N_HEADS):
        q0 = h * LANES + (0 if h % 2 == 0 else HEAD_DIM)
        table[0, q0:q0 + HEAD_DIM] = 1.0
        b = _extras_base(h) + BIAS_LANE
        table[1, b + 3:b + 6] = 1.0
        table[2, b:b + 3] = 1.0
        table[4, b:b + 3] = -1.0
        table[5, b + 3:b + 6] = 1.0
        table[6, b:b + 6] = [0, 1, 2, 0, 1, 2]
        m = _extras_base(h) + MASK_LANE
        table[3, m:m + MAX_MOBA_BLOCKS] = np.arange(MAX_MOBA_BLOCKS)
    slope_of_lane = slopes.astype(F32)[head_of] * LOG2E
    scale = jnp.ones((SUBLANES, AUG), F32).at[4].set(slope_of_lane).at[5].set(slope_of_lane)
    return jnp.asarray(table) * scale


def _placement_matrices():
    p_fox = np.zeros((LANES, 2 * AUG), np.float32)
    p_mask = np.zeros((N_HEADS * MAX_MOBA_BLOCKS, AUG), np.float32)
    for h in range(N_HEADS):
        b = _extras_base(h) + BIAS_LANE
        for k in range(3):
            p_fox[k * N_HEADS + h, b + k] = 1.0
            p_fox[k * N_HEADS + h, AUG + b + 3 + k] = -1.0
        m = _extras_base(h) + MASK_LANE
        for j in range(MAX_MOBA_BLOCKS):
            p_mask[h * MAX_MOBA_BLOCKS + j, m + j] = 1.0
    return jnp.asarray(p_fox, BF16), jnp.asarray(p_mask, BF16)


def _split3(v):
    hi = v.astype(BF16)
    r = v - hi.astype(F32)
    mid = r.astype(BF16)
    lo = (r - mid.astype(F32)).astype(BF16)
    return hi, mid, lo


def _split2(v):
    hi = v.astype(BF16)
    return hi, (v - hi.astype(F32)).astype(BF16)


def _rms(x, g):
    return x * lax.rsqrt(jnp.mean(x * x, axis=-1, keepdims=True) + EPS) * g


def _dot(a, b):
    return jnp.dot(a, b, preferred_element_type=F32)


def _dot_nt(a, b):
    return lax.dot_general(a, b, (((1,), (1,)), ((), ())), preferred_element_type=F32)


def _dup_pairs(t):
    cols = []
    for p in range(GROUP_WIDTH // LANES):
        c = t[:, p * LANES:(p + 1) * LANES]
        cols += [c, c]
    return jnp.concatenate(cols, axis=1)


def _const_spec(shape):
    return pl.BlockSpec(shape, lambda *_: (0,) * len(shape), pipeline_mode=pl.Buffered(1))


def _inproj_kernel(x_ref, g_ref, w_ref, bf_ref, tri_ref, pfox_ref, pmask_ref, meta_ref,
                   qf_ref, kf_ref, vf_ref, qm_ref, km_ref, vm_ref, carry_ref, kbdt_ref, *, blocks_per_step):
    tm = MOBA_BLOCK
    i = pl.program_id(1)

    @pl.when(i == 0)
    def _():
        carry_ref[...] = jnp.zeros_like(carry_ref)
        kbdt_ref[...] = jnp.zeros_like(kbdt_ref)

    meta = meta_ref[...]
    own = meta[0:1] > 0.5
    ones_q = meta[1:2]
    ones_k = meta[2:3]
    block_of_lane = meta[3:4]
    piece_of_lane = meta[6:7]
    head_lane = lax.broadcasted_iota(jnp.int32, (1, FLOGIT_PAD), 1) < N_HEADS
    tri = tri_ref[...]
    jrow = lax.broadcasted_iota(jnp.int32, (MAX_MOBA_BLOCKS, tm), 0)
    head_of_lane = lax.broadcasted_iota(jnp.int32, (1, GROUP_WIDTH), 1) // HEAD_DIM

    def project(rows):
        a = _rms(x_ref[0, rows, :], g_ref[...]).astype(BF16)
        cols = [_dot(a, w_ref[:, c * GROUP_WIDTH:(c + 1) * GROUP_WIDTH]) for c in range(6)]
        return cols + [_dot(a, w_ref[:, 6 * GROUP_WIDTH:6 * GROUP_WIDTH + FLOGIT_PAD])]

    def finish(rows, blk, projected):
        qf, kf, vf, qm, km, vm, z = projected

        z = z + bf_ref[...]
        log_f = jnp.minimum(z, 0.0) - jnp.log1p(jnp.exp(-jnp.abs(z)))
        f_hi, f_mid, f_lo = _split3(log_f)
        c = (_dot(tri, f_hi) + _dot(tri, f_mid) + _dot(tri, f_lo)) + carry_ref[...]
        carry_ref[...] = c[tm - 1:tm, :]
        c_hi, c_mid, c_lo = _split3(jnp.where(head_lane, c * LOG2E, 0.0))
        packed = (c_hi.astype(F32) + pltpu.roll(c_mid.astype(F32), N_HEADS, 1)
                  + pltpu.roll(c_lo.astype(F32), 2 * N_HEADS, 1)).astype(BF16)
        ext = _dot(packed, pfox_ref[...])
        qf_ref[0, rows, :] = jnp.where(own, _dup_pairs(qf * QK_SCALE), ext[:, :AUG] + ones_q).astype(BF16)
        kf_ref[0, rows, :] = jnp.where(own, _dup_pairs(kf), ext[:, AUG:] + ones_k).astype(BF16)
        vf_ref[0, rows, :] = vf.astype(BF16)

        vm_ref[0, rows, :] = vm.astype(BF16)
        kb_hi, kb_lo = _split2(kbdt_ref[...])
        q_hi, q_lo = _split2(qm)
        gate_t = _dot_nt(kb_hi, q_hi) + _dot_nt(kb_hi, q_lo) + _dot_nt(kb_lo, q_hi)
        past = jrow < blk
        masks = []
        for h in range(N_HEADS):
            g = jnp.where(past, gate_t[h * MAX_MOBA_BLOCKS:(h + 1) * MAX_MOBA_BLOCKS], -jnp.inf)
            sel = jrow == blk
            for _ in range(MOBA_TOP_K):
                top = jnp.max(g, axis=0, keepdims=True)
                cand = past & (g == top)
                first = jnp.min(jnp.where(cand, jrow, MAX_MOBA_BLOCKS), axis=0, keepdims=True)
                pick = jrow == first
                sel = sel | pick
                g = jnp.where(pick, -jnp.inf, g)
            masks.append(jnp.where(sel, 0.0, MASK_NEG))
        mask = jnp.concatenate(masks, axis=0).T.astype(BF16)
        ext_mask = _dot(mask, pmask_ref[...])

        pos = (blk * tm + lax.broadcasted_iota(jnp.int32, (tm, 1), 0)).astype(F32)

        def bias_pieces(full):
            hi = full.astype(BF16).astype(F32)
            r = full - hi
            mid = r.astype(BF16).astype(F32)
            return jnp.where(piece_of_lane == 0.0, hi, jnp.where(piece_of_lane == 1.0, mid, r - mid))

        onehot = jnp.where(block_of_lane == blk.astype(F32), 1.0, 0.0)
        qm_ref[0, rows, :] = jnp.where(own, _dup_pairs(qm * QK_SCALE),
                                       ext_mask + bias_pieces(pos * meta[4:5]) + ones_q).astype(BF16)
        km_ref[0, rows, :] = jnp.where(own, _dup_pairs(km),
                                       bias_pieces(pos * meta[5:6]) + ones_k + onehot).astype(BF16)

        kmean = jnp.mean(km, axis=0, keepdims=True)
        for h in range(N_HEADS):
            kbdt_ref[pl.ds(h * MAX_MOBA_BLOCKS + blk, 1), :] = jnp.where(head_of_lane == h, kmean, 0.0)

    row_groups = [pl.ds(g * tm, tm) for g in range(blocks_per_step)]
    ahead = project(row_groups[0])
    for g in range(blocks_per_step):
        current = ahead
        if g + 1 < blocks_per_step:
            ahead = project(row_groups[g + 1])
        finish(row_groups[g], i * blocks_per_step + g, current)


def _in_projection(x, g_mix, w_in, b_forget, slopes):
    B, S, D = x.shape
    n_blk = S // MOBA_BLOCK
    bps = IN_BLOCKS_PER_STEP if n_blk % IN_BLOCKS_PER_STEP == 0 else 1
    tm = bps * MOBA_BLOCK
    assert S % MOBA_BLOCK == 0 and MOBA_TOP_K <= n_blk <= MAX_MOBA_BLOCKS and D == D_MODEL
    c = np.cumsum([GROUP_WIDTH, GROUP_WIDTH, GROUP_WIDTH, N_HEADS, GROUP_WIDTH, GROUP_WIDTH])
    w_f = jnp.pad(w_in[:, c[2]:c[3]], ((0, 0), (0, FLOGIT_PAD - N_HEADS)))
    w_r = jnp.concatenate([w_in[:, :c[2]], w_in[:, c[3]:], w_f], axis=1).astype(BF16)
    n_cols = w_r.shape[1]
    b_f = jnp.pad(b_forget.astype(F32), (0, FLOGIT_PAD - N_HEADS)).reshape(1, FLOGIT_PAD)
    tri = jnp.asarray(np.tril(np.ones((MOBA_BLOCK, MOBA_BLOCK), np.float32)), BF16)
    p_fox, p_mask = _placement_matrices()
    meta = _lane_tables(slopes)

    aug = jax.ShapeDtypeStruct((B, S, AUG), BF16)
    val = jax.ShapeDtypeStruct((B, S, GROUP_WIDTH), BF16)
    tile = lambda w: pl.BlockSpec((1, tm, w), lambda b, i: (b, i, 0))
    return pl.pallas_call(
        functools.partial(_inproj_kernel, blocks_per_step=bps),
        grid=(B, n_blk // bps),
        in_specs=[tile(D), _const_spec((1, D)), _const_spec((D, n_cols)), _const_spec((1, FLOGIT_PAD)),
                  _const_spec(tri.shape), _const_spec(p_fox.shape),
                  _const_spec(p_mask.shape), _const_spec(meta.shape)],
        out_specs=[tile(AUG), tile(AUG), tile(GROUP_WIDTH), tile(AUG), tile(AUG), tile(GROUP_WIDTH)],
        out_shape=[aug, aug, val, aug, aug, val],
        scratch_shapes=[pltpu.VMEM((1, FLOGIT_PAD), F32),
                        pltpu.VMEM((N_HEADS * MAX_MOBA_BLOCKS, GROUP_WIDTH), F32)],
        compiler_params=pltpu.CompilerParams(dimension_semantics=("arbitrary", "arbitrary"),
                                             vmem_limit_bytes=VMEM_LIMIT),
        name="in_projection",
    )(x, g_mix.reshape(1, D), w_r, b_f, tri, p_fox, p_mask, meta)


def _attention_kernel(q_ref, k_ref, v_ref, o_ref, vt_ref, s_ref, p_ref, alpha_ref, m_ref, acc_ref, *, seq, t):
    nq = seq // t
    low = lax.broadcasted_iota(jnp.int32, (1, LANES), 1) < HEAD_DIM
    one = jnp.ones((), F32)

    def rows(tile):
        return pl.ds(pl.multiple_of(tile * t, t), t)

    def transpose_values(tile, carry):
        vp = v_ref[0, rows(tile), :].astype(F32)
        vt_ref[0, tile] = jnp.where(low, vp, one).T.astype(BF16)
        vt_ref[1, tile] = jnp.where(low, one, vp).T.astype(BF16)
        return carry

    lax.fori_loop(0, nq, transpose_values, 0, unroll=SWEEP_UNROLL)
    causal = (lax.broadcasted_iota(jnp.int32, (t, t), 0) <= lax.broadcasted_iota(jnp.int32, (t, t), 1))

    def stage_a(qi, kt, slot, diag):
        for hh in range(2):
            cols = slice(hh * LANES, (hh + 1) * LANES)
            s = _dot_nt(k_ref[0, rows(kt), cols], q_ref[0, rows(qi), cols])
            s_ref[slot, hh] = jnp.where(causal, s, -jnp.inf) if diag else s

    def stage_b(qi, kt, slot, diag):
        for hh in range(2):
            s = s_ref[slot, hh]
            m_cur = jnp.max(s, axis=0, keepdims=True)
            if diag:
                m_next = m_cur
            else:
                m_prev = m_ref[hh, qi]
                m_next = jnp.maximum(m_prev, m_cur)
                alpha_ref[slot, hh] = jnp.exp2(m_prev - m_next)
            m_ref[hh, qi] = m_next
            p_ref[slot, hh] = jnp.exp2(s - m_next).astype(BF16)

    def stage_c(qi, kt, slot, diag):
        for hh in range(2):
            pv = _dot(vt_ref[hh, kt], p_ref[slot, hh])
            acc_ref[hh, qi] = pv if diag else alpha_ref[slot, hh] * acc_ref[hh, qi] + pv

    def run_pipeline(n_steps, first, advance, diag, unroll):
        def step(parity, cs):
            ca, cb, cc = cs
            stage_a(*ca, parity, diag)
            stage_b(*cb, 1 - parity, diag)
            stage_c(*cc, parity, diag)
            return advance(*ca), advance(*cb), advance(*cc)

        def trip(_, cs):
            for u in range(unroll):
                cs = step(u % 2, cs)
            return cs

        c0 = first
        c1 = advance(*c0)
        stage_a(*c0, 0, diag)
        stage_a(*c1, 1, diag)
        stage_b(*c0, 0, diag)
        cs = (advance(*c1), c1, c0)
        n_steady = n_steps - 2
        cs = lax.fori_loop(0, n_steady // unroll, trip, cs)
        f = 2 + unroll * (n_steady // unroll)
        for _ in range(n_steady % unroll):
            cs = step(f % 2, cs)
            f += 1
        _, cb, cc = cs
        stage_b(*cb, (f - 1) % 2, diag)
        stage_c(*cc, f % 2, diag)
        stage_c(*advance(*cc), (f + 1) % 2, diag)

    def next_below_diagonal(qi, kt):
        wrap = kt + 1 >= qi
        return jnp.where(wrap, qi + 1, qi), jnp.where(wrap, 0, kt + 1)

    i32 = jnp.int32
    run_pipeline(nq, (i32(0), i32(0)), lambda qi, kt: (qi + 1, kt + 1), diag=True, unroll=2)
    run_pipeline(nq * (nq - 1) // 2, (i32(1), i32(0)), next_below_diagonal, diag=False, unroll=STEPS_PER_TRIP)

    def normalise(tile, carry):
        a0 = acc_ref[0, tile]
        a1 = acc_ref[1, tile]
        out_t = jnp.concatenate([a0[:HEAD_DIM] / a0[HEAD_DIM:HEAD_DIM + 1], a1[HEAD_DIM:] / a1[0:1]], axis=0)
        o_ref[0, rows(tile), :] = out_t.T
        return carry

    lax.fori_loop(0, nq, normalise, 0, unroll=SWEEP_UNROLL)


def _attention(q_aug, k_aug, v):
    B, S, _ = q_aug.shape
    t = ATTN_TILE if S % ATTN_TILE == 0 and S // ATTN_TILE >= 3 else MOBA_BLOCK
    assert S % t == 0 and S // t >= 3
    pair = lambda w: pl.BlockSpec((1, S, w), lambda b, p: (b, 0, p))
    return pl.pallas_call(
        functools.partial(_attention_kernel, seq=S, t=t),
        grid=(B, N_HEADS // 2),
        in_specs=[pair(2 * LANES), pair(2 * LANES), pair(LANES)],
        out_specs=pair(LANES),
        out_shape=jax.ShapeDtypeStruct((B, S, GROUP_WIDTH), F32),
        scratch_shapes=[pltpu.VMEM((2, S // t, LANES, t), BF16),
                        pltpu.VMEM((2, 2, t, t), F32),
                        pltpu.VMEM((2, 2, t, t), BF16),
                        pltpu.VMEM((2, 2, 1, t), F32),
                        pltpu.VMEM((2, S // t, 1, t), F32),
                        pltpu.VMEM((2, S // t, LANES, t), F32)],
        compiler_params=pltpu.CompilerParams(dimension_semantics=("arbitrary", "arbitrary"),
                                             vmem_limit_bytes=VMEM_LIMIT),
        name="pair_attention",
    )(q_aug, k_aug, v)


def _mem_kv_kernel(mem_ref, g_ref, w_ref, kv_ref):
    kv_ref[0] = _dot(_rms(mem_ref[0], g_ref[...]).astype(BF16), w_ref[...]).astype(BF16)


def _memory_kv(mem, g_mem, w_kv):
    B, M, D = mem.shape
    return pl.pallas_call(
        _mem_kv_kernel,
        grid=(B,),
        in_specs=[pl.BlockSpec((1, M, D), lambda b: (b, 0, 0)), _const_spec((1, D)), _const_spec((D, 2 * D))],
        out_specs=pl.BlockSpec((1, M, 2 * D), lambda b: (b, 0, 0)),
        out_shape=jax.ShapeDtypeStruct((B, M, 2 * D), BF16),
        compiler_params=pltpu.CompilerParams(dimension_semantics=("arbitrary",), vmem_limit_bytes=VMEM_LIMIT),
        name="memory_kv",
    )(mem, g_mem.reshape(1, D), w_kv.astype(BF16))


def _post_kernel(of_ref, om_ref, x_ref, kv_ref, gfox_ref, gmoba_ref, gcross_ref, gffn_ref, gfin_ref,
                 wout_ref, wq_ref, wo_ref, wgu_ref, cw_ref, wd_ref, out_ref,
                 halo_ref, *, tm, n_chunks, final_norm):
    i = pl.program_id(1)

    @pl.when(i == 0)
    def _():
        halo_ref[...] = jnp.zeros_like(halo_ref)

    yf = _rms(of_ref[0], gfox_ref[...]).astype(BF16)
    ym = _rms(om_ref[0], gmoba_ref[...]).astype(BF16)
    h = x_ref[0] + _dot(yf, wout_ref[0:GROUP_WIDTH, :]) + _dot(ym, wout_ref[GROUP_WIDTH:2 * GROUP_WIDTH, :])

    qc = (_dot(_rms(h, gcross_ref[...]).astype(BF16), wq_ref[...]) * (CROSS_HEAD_DIM ** -0.5)).astype(BF16)
    heads = []
    for hd in range(N_CROSS_HEADS):
        lo = hd * CROSS_HEAD_DIM
        s = _dot_nt(qc[:, lo:lo + CROSS_HEAD_DIM], kv_ref[0, :, lo:lo + CROSS_HEAD_DIM])
        p = jnp.exp(s - jnp.max(s, axis=1, keepdims=True))
        o = _dot(p.astype(BF16), kv_ref[0, :, D_MODEL + lo:D_MODEL + lo + CROSS_HEAD_DIM])
        heads.append((o / jnp.sum(p, axis=1, keepdims=True)).astype(BF16))
    h = h + _dot(jnp.concatenate(heads, axis=1), wo_ref[...])

    a3 = _rms(h, gffn_ref[...]).astype(BF16)
    first_rows = lax.broadcasted_iota(jnp.int32, (SUBLANES, FFN_CHUNK), 0)

    def delayed(g, prev_tail, d):
        rolled = pltpu.roll(g, d, 0)
        head = jnp.where(first_rows < d, pltpu.roll(prev_tail, d, 0), rolled[:SUBLANES])
        return jnp.concatenate([head, rolled[SUBLANES:]], axis=0)

    gu_next = _dot(a3, wgu_ref[0])
    for c in range(n_chunks):
        gu = gu_next
        if c + 1 < n_chunks:
            gu_next = _dot(a3, wgu_ref[c + 1])
        g = gu[:, :FFN_CHUNK]
        prev_tail = halo_ref[c]
        halo_ref[c] = g[tm - SUBLANES:tm, :]
        cw = cw_ref[c]
        gc = cw[3:4] + cw[0:1] * delayed(g, prev_tail, 2) + cw[1:2] * delayed(g, prev_tail, 1) + cw[2:3] * g
        half = 0.5 * gc
        act = (half * (1.0 + jnp.tanh(half)) * gu[:, FFN_CHUNK:]).astype(BF16)
        h = h + _dot(act, wd_ref[c])
    out_ref[0] = _rms(h, gfin_ref[...]) if final_norm else h


def _post_attention(o_f, o_m, x, kv, g_fox, g_moba, w_out, g_cross, w_q, w_o, g_ffn, w_up, conv_w, conv_b,
                    w_down, g_final, final_norm):
    B, S, D = x.shape
    M = kv.shape[1]
    tm = POST_TILE
    d_ff = w_down.shape[0]
    assert d_ff % FFN_CHUNK == 0 and S % tm == 0
    nc = d_ff // FFN_CHUNK
    w_g = w_up[:, :d_ff].reshape(D, nc, FFN_CHUNK)
    w_u = w_up[:, d_ff:].reshape(D, nc, FFN_CHUNK)
    w_gu = jnp.concatenate([w_g, w_u], axis=2).transpose(1, 0, 2).astype(BF16)
    w_d = w_down.reshape(nc, FFN_CHUNK, D).astype(BF16)
    cw = jnp.concatenate([conv_w, conv_b[None, :], jnp.zeros((SUBLANES - CONV_WIDTH - 1, d_ff), F32)], axis=0)
    cw = cw.reshape(SUBLANES, nc, FFN_CHUNK).transpose(1, 0, 2)
    row = lambda g: g.reshape(1, -1).astype(F32)
    tile = lambda w: pl.BlockSpec((1, tm, w), lambda b, i: (b, i, 0))
    return pl.pallas_call(
        functools.partial(_post_kernel, tm=tm, n_chunks=nc, final_norm=final_norm),
        grid=(B, S // tm),
        in_specs=[tile(GROUP_WIDTH), tile(GROUP_WIDTH), tile(D),
                  pl.BlockSpec((1, M, 2 * D), lambda b, i: (b, 0, 0)),
                  _const_spec((1, GROUP_WIDTH)), _const_spec((1, GROUP_WIDTH)), _const_spec((1, D)),
                  _const_spec((1, D)), _const_spec((1, D)),
                  _const_spec((2 * GROUP_WIDTH, D)), _const_spec((D, D)), _const_spec((D, D)),
                  _const_spec(w_gu.shape), _const_spec(cw.shape), _const_spec(w_d.shape)],
        out_specs=tile(D),
        out_shape=jax.ShapeDtypeStruct((B, S, D), F32),
        scratch_shapes=[pltpu.VMEM((nc, SUBLANES, FFN_CHUNK), F32)],
        compiler_params=pltpu.CompilerParams(dimension_semantics=("arbitrary", "arbitrary"),
                                             vmem_limit_bytes=VMEM_LIMIT),
        name="post_attention",
    )(o_f, o_m, x, kv, row(g_fox), row(g_moba), row(g_cross), row(g_ffn), row(g_final),
      w_out.astype(BF16), w_q.astype(BF16), w_o.astype(BF16), w_gu, cw, w_d)


def kernel(x, mem, g_mix, w_in, b_forget, g_fox, g_moba, w_out, g_cross, g_mem, w_q_mem, w_kv_mem,
           w_o_mem, g_ffn, w_up, conv_w, conv_b, w_down, g_final):
    depth = g_mix.shape[0]
    slopes = 2.0 ** (-8.0 * jnp.arange(1, N_HEADS + 1, dtype=F32) / N_HEADS)
    h = x
    for l in range(depth):
        qf, kf, vf, qm, km, vm = _in_projection(h, g_mix[l], w_in[l], b_forget[l], slopes)
        o_f = _attention(qf, kf, vf)
        o_m = _attention(qm, km, vm)
        kv = _memory_kv(mem, g_mem[l], w_kv_mem[l])
        h = _post_attention(o_f, o_m, h, kv, g_fox[l], g_moba[l], w_out[l], g_cross[l], w_q_mem[l],
                            w_o_mem[l], g_ffn[l], w_up[l], conv_w[l], conv_b[l], w_down[l], g_final,
                            final_norm=(l == depth - 1))
    return h
```
